```python
import math
import jax
import jax.numpy as jnp
from jax import lax
import numpy as np

D_MODEL = 1024
BATCH = 8
SEQ = 2048
DEPTH = 4

CTX_LEN = 256
GRID_W = 64
N_EVEN = (DEPTH + 1) // 2
N_ODD = DEPTH // 2
NORM_EPS = 1e-6
F32 = jnp.float32

LRU_WIDTH = D_MODEL
LRU_HEADS = 8
LRU_HEAD_DIM = LRU_WIDTH // LRU_HEADS
LRU_CONV = 4
LRU_C = 8.0

S5_WIDTH = D_MODEL // 2
S5_GROUP = 16
S5_GROUPS = S5_WIDTH // S5_GROUP
S5_STATE = 64
S5_DT_MIN = 1e-3
S5_DT_MAX = 1e-1

EVEN_SPLITS = (LRU_WIDTH, 2 * LRU_WIDTH, 2 * LRU_WIDTH + S5_WIDTH)
EVEN_IN = 2 * LRU_WIDTH + 2 * S5_WIDTH
EVEN_MIX = LRU_WIDTH + S5_WIDTH

MLA_HEADS = 8
MLA_Q_RANK = 384
MLA_KV_RANK = 256
MLA_NOPE = 128
MLA_ROPE = 64
MLA_V = 128
MLA_WIDTH = MLA_HEADS * MLA_V
MLA_SPLITS = (MLA_Q_RANK, MLA_Q_RANK + MLA_KV_RANK, MLA_Q_RANK + MLA_KV_RANK + MLA_ROPE)
MLA_IN = MLA_Q_RANK + MLA_KV_RANK + MLA_ROPE + MLA_WIDTH
MLA_SCALE = 1.0 / math.sqrt(MLA_NOPE + MLA_ROPE)
ROPE_AXIS = MLA_ROPE // 2
ROPE_BASE = 10000.0
Q_BLOCK = 128

kernel_name = 'hybrid_rglru_s5_mla_prefix_dit'


def rmsnorm(x, g):
    xf = x.astype(F32)
    y = xf * lax.rsqrt(jnp.mean(xf * xf, axis=-1, keepdims=True) + NORM_EPS)
    return (y * g.astype(F32)).astype(x.dtype)


def modulation(cond, w, b):
    return jnp.split(jax.nn.silu(cond) @ w + b, 3, axis=-1)


def dwconv_centred(u, w, b):
    L = u.shape[1]
    up = jnp.pad(u, ((0, 0), (LRU_CONV // 2, LRU_CONV - 1 - LRU_CONV // 2), (0, 0)))
    out = b
    for k in range(LRU_CONV):
        out = out + up[:, k:k + L] * w[k]
    return out


def _real_combine(e1, e2):
    a1, b1 = e1
    a2, b2 = e2
    return a1 * a2, a2 * b1 + b2


def real_scan(a, b, h0, reverse):
    a_cum, h = lax.associative_scan(_real_combine, (a, b), reverse=reverse, axis=1)
    if h0 is not None:
        h = h + a_cum * h0[:, None]
    return h


def _complex_combine(e1, e2):
    ar1, ai1, br1, bi1 = e1
    ar2, ai2, br2, bi2 = e2
    return (ar1 * ar2 - ai1 * ai2, ar1 * ai2 + ai1 * ar2,
            ar2 * br1 - ai2 * bi1 + br2, ar2 * bi1 + ai2 * br1 + bi2)


def complex_scan(a_re, a_im, b_re, b_im, h0, reverse):
    a_re = jnp.broadcast_to(a_re, b_re.shape)
    a_im = jnp.broadcast_to(a_im, b_re.shape)
    ar, ai, hr, hi = lax.associative_scan(_complex_combine, (a_re, a_im, b_re, b_im), reverse=reverse, axis=1)
    if h0 is not None:
        h0r, h0i = h0[0][:, None], h0[1][:, None]
        hr, hi = hr + ar * h0r - ai * h0i, hi + ar * h0i + ai * h0r
    return hr, hi


def rglru_coeffs(u, wr, br, wi, bi, lam):
    uh = u.reshape(*u.shape[:-1], LRU_HEADS, LRU_HEAD_DIM)
    r = jax.nn.sigmoid(jnp.einsum('blhi,hij->blhj', uh, wr.astype(F32)).reshape(u.shape) + br.astype(F32))
    i = jax.nn.sigmoid(jnp.einsum('blhi,hij->blhj', uh, wi.astype(F32)).reshape(u.shape) + bi.astype(F32))
    log_a = -LRU_C * r * jax.nn.softplus(-lam.astype(F32))
    a = jnp.exp(log_a)
    b = jnp.sqrt(-jnp.expm1(2.0 * log_a)) * (i * u)
    return a, b


def rglru_mix(x_ctx, x_lat, conv_w, conv_b, wr, br, wi, bi, lam, with_ctx):
    u_ctx = dwconv_centred(x_ctx, conv_w, conv_b).astype(F32)
    u_lat = dwconv_centred(x_lat, conv_w, conv_b).astype(F32)
    hs_ctx, hs_lat = [], []
    for d, rev in enumerate((False, True)):
        a_c, b_c = rglru_coeffs(u_ctx, wr[d], br[d], wi[d], bi[d], lam[d])
        h_c = real_scan(a_c, b_c, None, rev)
        h0 = h_c[:, 0] if rev else h_c[:, -1]
        a_l, b_l = rglru_coeffs(u_lat, wr[d], br[d], wi[d], bi[d], lam[d])
        hs_lat.append(real_scan(a_l, b_l, h0, rev))
        hs_ctx.append(h_c)
    y_lat = hs_lat[0] + hs_lat[1]
    y_ctx = hs_ctx[0] + hs_ctx[1] if with_ctx else None
    return y_ctx, y_lat


def s5_discretise(lam_re, lam_im, log_dt, b_re, b_im):
    lam_re = lam_re.astype(F32)
    lam_im = lam_im.astype(F32)
    dt = jnp.exp(log_dt.astype(F32))
    mag = jnp.exp(lam_re * dt)
    ab_re = mag * jnp.cos(lam_im * dt)
    ab_im = mag * jnp.sin(lam_im * dt)
    den = lam_re * lam_re + lam_im * lam_im
    nr = ab_re - 1.0
    f_re = (nr * lam_re + ab_im * lam_im) / den
    f_im = (ab_im * lam_re - nr * lam_im) / den
    b_re = b_re.astype(F32)
    b_im = b_im.astype(F32)
    bb_re = f_re[..., None] * b_re - f_im[..., None] * b_im
    bb_im = f_re[..., None] * b_im + f_im[..., None] * b_re
    return ab_re, ab_im, bb_re, bb_im


def s5_mix(u_ctx, u_lat, lam_re, lam_im, log_dt, b_re, b_im, c_re, c_im, d_skip, glu_w, glu_b, with_ctx):
    dtype = u_lat.dtype
    uc = u_ctx.astype(F32).reshape(*u_ctx.shape[:-1], S5_GROUPS, S5_GROUP)
    ul = u_lat.astype(F32).reshape(*u_lat.shape[:-1], S5_GROUPS, S5_GROUP)
    ys_ctx, ys_lat = [], []
    for d, rev in enumerate((False, True)):
        ab_re, ab_im, bb_re, bb_im = s5_discretise(lam_re[d], lam_im[d], log_dt[d], b_re[d], b_im[d])
        cr, ci = c_re[d].astype(F32), c_im[d].astype(F32)
        hc_re, hc_im = complex_scan(ab_re, ab_im,
                                    jnp.einsum('blgh,gph->blgp', uc, bb_re),
                                    jnp.einsum('blgh,gph->blgp', uc, bb_im), None, rev)
        idx = 0 if rev else -1
        hl_re, hl_im = complex_scan(ab_re, ab_im,
                                    jnp.einsum('blgh,gph->blgp', ul, bb_re),
                                    jnp.einsum('blgh,gph->blgp', ul, bb_im),
                                    (hc_re[:, idx], hc_im[:, idx]), rev)
        ys_lat.append(jnp.einsum('ghp,blgp->blgh', cr, hl_re) - jnp.einsum('ghp,blgp->blgh', ci, hl_im))
        if with_ctx:
            ys_ctx.append(jnp.einsum('ghp,blgp->blgh', cr, hc_re) - jnp.einsum('ghp,blgp->blgh', ci, hc_im))

    def finish(ys, u):
        y = ys[0] + ys[1] + d_skip.astype(F32) * u
        y = jax.nn.gelu(y.reshape(*y.shape[:2], S5_WIDTH)).astype(dtype)
        return y * jax.nn.sigmoid(y @ glu_w + glu_b)

    y_lat = finish(ys_lat, ul)
    y_ctx = finish(ys_ctx, uc) if with_ctx else None
    return y_ctx, y_lat


def even_mix(h_ctx, h_lat, w_in, conv_w, conv_b, wr, br, wi, bi, lam, lam_re, lam_im, log_dt,
             b_re, b_im, c_re, c_im, d_skip, glu_w, glu_b, w_out, with_ctx):
    xa_c, ga_c, ub_c, gb_c = jnp.split(h_ctx @ w_in, EVEN_SPLITS, axis=-1)
    xa_l, ga_l, ub_l, gb_l = jnp.split(h_lat @ w_in, EVEN_SPLITS, axis=-1)
    ya_c, ya_l = rglru_mix(xa_c, xa_l, conv_w, conv_b, wr, br, wi, bi, lam, with_ctx)
    yb_c, yb_l = s5_mix(ub_c, ub_l, lam_re, lam_im, log_dt, b_re, b_im, c_re, c_im, d_skip, glu_w, glu_b, with_ctx)

    def merge(ya, ga, yb, gb):
        return jnp.concatenate([ya.astype(ga.dtype) * jax.nn.silu(ga), yb * jax.nn.silu(gb)], axis=-1) @ w_out

    out_lat = merge(ya_l, ga_l, yb_l, gb_l)
    out_ctx = merge(ya_c, ga_c, yb_c, gb_c) if with_ctx else None
    return out_ctx, out_lat


def axial_rope(n_tokens):
    rows = n_tokens // GRID_W
    row = jnp.repeat(jnp.arange(rows, dtype=F32), GRID_W)
    col = jnp.tile(jnp.arange(GRID_W, dtype=F32), rows)
    inv = ROPE_BASE ** (-jnp.arange(0, ROPE_AXIS, 2, dtype=F32) / ROPE_AXIS)
    ang = jnp.concatenate([row[:, None] * inv, col[:, None] * inv], axis=-1)
    return jnp.cos(ang), jnp.sin(ang)


def apply_rope(x, cos, sin):
    x1, x2 = jnp.split(x.astype(F32), 2, axis=-1)
    return jnp.concatenate([x1 * cos - x2 * sin, x2 * cos + x1 * sin], axis=-1).astype(x.dtype)


def attend(qn, qr, kn, kr, v):
    s = jnp.einsum('bqhd,bkhd->bhqk', qn, kn) + jnp.einsum('bqhr,bkr->bhqk', qr, kr)
    p = jax.nn.softmax(s.astype(F32) * MLA_SCALE, axis=-1)
    return jnp.einsum('bhqk,bkhd->bqhd', p.astype(v.dtype), v)


def mla_mix(h_ctx, h_lat, w_in, q_norm, w_uq, kv_norm, w_ukv, w_out, with_ctx):
    def project(h):
        cq, ckv, kr, gate = jnp.split(h @ w_in, MLA_SPLITS, axis=-1)
        q = (rmsnorm(cq, q_norm) @ w_uq).reshape(*h.shape[:2], MLA_HEADS, MLA_NOPE + MLA_ROPE)
        kv = (rmsnorm(ckv, kv_norm) @ w_ukv).reshape(*h.shape[:2], MLA_HEADS, MLA_NOPE + MLA_V)
        return q[..., :MLA_NOPE], q[..., MLA_NOPE:], kv[..., :MLA_NOPE], kr, kv[..., MLA_NOPE:], gate

    qn_c, qr_c, kn_c, kr_c, v_c, g_c = project(h_ctx)
    qn_l, qr_l, kn_l, kr_l, v_l, g_l = project(h_lat)
    bsz, n_lat = h_lat.shape[0], h_lat.shape[1]
    cos, sin = axial_rope(n_lat)
    qr_l = apply_rope(qr_l, cos[:, None], sin[:, None])
    kr_l = apply_rope(kr_l, cos, sin)
    kn = jnp.concatenate([kn_c, kn_l], axis=1)
    kr = jnp.concatenate([kr_c, kr_l], axis=1)
    v = jnp.concatenate([v_c, v_l], axis=1)
    nb = n_lat // Q_BLOCK
    qn_b = qn_l.reshape(bsz, nb, Q_BLOCK, MLA_HEADS, MLA_NOPE).swapaxes(0, 1)
    qr_b = qr_l.reshape(bsz, nb, Q_BLOCK, MLA_HEADS, MLA_ROPE).swapaxes(0, 1)
    o_l = lax.map(lambda qs: attend(qs[0], qs[1], kn, kr, v), (qn_b, qr_b))
    o_l = o_l.swapaxes(0, 1).reshape(bsz, n_lat, MLA_WIDTH)
    out_lat = (o_l * jax.nn.silu(g_l)) @ w_out
    out_ctx = None
    if with_ctx:
        o_c = attend(qn_c, qr_c, kn_c, kr_c, v_c).reshape(bsz, h_ctx.shape[1], MLA_WIDTH)
        out_ctx = (o_c * jax.nn.silu(g_c)) @ w_out
    return out_ctx, out_lat


def setup_inputs(seed: int = 0) -> dict:
    key = jax.random.key(seed)
    ks = iter(jax.random.split(key, 48))

    def nrm(shape, scale):
        return jax.random.normal(next(ks), shape, F32) * scale

    def gain(shape):
        return 1.0 + nrm(shape, 0.05)

    D = D_MODEL
    G, P, H = S5_GROUPS, S5_STATE, S5_GROUP
    v = jax.random.uniform(next(ks), (N_EVEN, 2, LRU_WIDTH), F32, minval=0.9, maxval=0.999)
    a0 = v ** (1.0 / LRU_C)
    lru_lam = jnp.log(a0) - jnp.log1p(-a0)
    s5_log_dt = jax.random.uniform(next(ks), (N_EVEN, 2, G, P), F32,
                                   minval=math.log(S5_DT_MIN), maxval=math.log(S5_DT_MAX))
    return {
        'x': nrm((BATCH, SEQ, D), 1.0),
        'c': nrm((BATCH, D), 1.0),
        'ctx': nrm((BATCH, CTX_LEN, D), 1.0),
        'c_ctx': nrm((D,), 1.0),
        'norm_g': gain((DEPTH, D)),
        'mod_w': nrm((DEPTH, D, 3 * D), 0.5 * D ** -0.5),
        'mod_b': nrm((DEPTH, 3 * D), 0.01),
        'ev_w_in': nrm((N_EVEN, D, EVEN_IN), D ** -0.5),
        'lru_conv_w': nrm((N_EVEN, LRU_CONV, LRU_WIDTH), LRU_CONV ** -0.5),
        'lru_conv_b': nrm((N_EVEN, LRU_WIDTH), 0.01),
        'lru_wr': nrm((N_EVEN, 2, LRU_HEADS, LRU_HEAD_DIM, LRU_HEAD_DIM), LRU_HEAD_DIM ** -0.5),
        'lru_br': nrm((N_EVEN, 2, LRU_WIDTH), 0.01),
        'lru_wi': nrm((N_EVEN, 2, LRU_HEADS, LRU_HEAD_DIM, LRU_HEAD_DIM), LRU_HEAD_DIM ** -0.5),
        'lru_bi': nrm((N_EVEN, 2, LRU_WIDTH), 0.01),
        'lru_lam': lru_lam,
        's5_lam_re': -0.5 * jnp.exp(nrm((N_EVEN, 2, G, P), 0.05)),
        's5_lam_im': jnp.pi * jnp.arange(P, dtype=F32) + nrm((N_EVEN, 2, G, P), 0.01),
        's5_log_dt': s5_log_dt,
        's5_b_re': nrm((N_EVEN, 2, G, P, H), (2.0 * H) ** -0.5),
        's5_b_im': nrm((N_EVEN, 2, G, P, H), (2.0 * H) ** -0.5),
        's5_c_re': nrm((N_EVEN, 2, G, H, P), P ** -0.5),
        's5_c_im': nrm((N_EVEN, 2, G, H, P), P ** -0.5),
        's5_d': nrm((N_EVEN, G, H), 1.0),
        's5_glu_w': nrm((N_EVEN, S5_WIDTH, S5_WIDTH), S5_WIDTH ** -0.5),
        's5_glu_b': nrm((N_EVEN, S5_WIDTH), 0.01),
        'ev_w_out': nrm((N_EVEN, EVEN_MIX, D), EVEN_MIX ** -0.5),
        'mla_w_in': nrm((N_ODD, D, MLA_IN), D ** -0.5),
        'mla_q_norm': gain((N_ODD, MLA_Q_RANK)),
        'mla_w_uq': nrm((N_ODD, MLA_Q_RANK, MLA_HEADS * (MLA_NOPE + MLA_ROPE)), MLA_Q_RANK ** -0.5),
        'mla_kv_norm': gain((N_ODD, MLA_KV_RANK)),
        'mla_w_ukv': nrm((N_ODD, MLA_KV_RANK, MLA_HEADS * (MLA_NOPE + MLA_V)), MLA_KV_RANK ** -0.5),
        'mla_w_out': nrm((N_ODD, MLA_WIDTH, D), MLA_WIDTH ** -0.5),
        'final_g': gain((D,)),
    }


def reference(x, c, ctx, c_ctx, norm_g, mod_w, mod_b,
              ev_w_in, lru_conv_w, lru_conv_b, lru_wr, lru_br, lru_wi, lru_bi, lru_lam,
              s5_lam_re, s5_lam_im, s5_log_dt, s5_b_re, s5_b_im, s5_c_re, s5_c_im, s5_d,
              s5_glu_w, s5_glu_b, ev_w_out,
              mla_w_in, mla_q_norm, mla_w_uq, mla_kv_norm, mla_w_ukv, mla_w_out,
              final_g):
    ctx_s = ctx
    for l in range(DEPTH):
        with_ctx = l < DEPTH - 1
        sh_l, sc_l, gt_l = modulation(c, mod_w[l], mod_b[l])
        sh_c, sc_c, gt_c = modulation(c_ctx, mod_w[l], mod_b[l])
        n_lat = rmsnorm(x, norm_g[l]) * (1.0 + sc_l[:, None]) + sh_l[:, None]
        n_ctx = rmsnorm(ctx_s, norm_g[l]) * (1.0 + sc_c) + sh_c
        if l % 2 == 0:
            e = l // 2
            o_ctx, o_lat = even_mix(n_ctx, n_lat, ev_w_in[e], lru_conv_w[e], lru_conv_b[e],
                                    lru_wr[e], lru_br[e], lru_wi[e], lru_bi[e], lru_lam[e],
                                    s5_lam_re[e], s5_lam_im[e], s5_log_dt[e], s5_b_re[e], s5_b_im[e],
                                    s5_c_re[e], s5_c_im[e], s5_d[e], s5_glu_w[e], s5_glu_b[e],
                                    ev_w_out[e], with_ctx)
        else:
            o = l // 2
            o_ctx, o_lat = mla_mix(n_ctx, n_lat, mla_w_in[o], mla_q_norm[o], mla_w_uq[o],
                                   mla_kv_norm[o], mla_w_ukv[o], mla_w_out[o], with_ctx)
        x = x + gt_l[:, None] * o_lat
        if with_ctx:
            ctx_s = ctx_s + gt_c * o_ctx
    return rmsnorm(x, final_g)
```

```python
import functools
import math

import jax
import jax.numpy as jnp
from jax import lax
from jax.experimental import pallas as pl
from jax.experimental.pallas import tpu as pltpu

F32 = jnp.float32
BF16 = jnp.bfloat16

NORM_EPS = 1e-6
LRU_C = 8.0
LRU_HEAD_DIM = 128
S5_GROUP = 16
S5_STATE = 64
MLA_HEADS = 8
MLA_NOPE = 128
MLA_ROPE = 64
MLA_V = 128
MLA_SCALE = 1.0 / math.sqrt(MLA_NOPE + MLA_ROPE)
ROPE_BASE = 10000.0
GRID_W = 64

LANE = 128
SUBLANE = 8
SCAN_STEPS = 32
MIB = 1024 * 1024


def _cparams(vmem_mib, *sem):
    return pltpu.CompilerParams(dimension_semantics=tuple(sem), vmem_limit_bytes=int(vmem_mib * MIB))


def _silu(x):
    return x * jax.nn.sigmoid(x)


def _mod_kernel(c_ref, w_ref, b_ref, o_ref):
    a = _silu(c_ref[...])
    o_ref[...] = jnp.dot(a, w_ref[...], preferred_element_type=F32, precision=lax.Precision.HIGHEST) + b_ref[...]


def _modulation(cond, mod_w, mod_b):
    depth, d, n = mod_w.shape
    rows = cond.shape[0]
    tn = min(n, 1024)
    return pl.pallas_call(
        _mod_kernel,
        grid=(depth, n // tn),
        in_specs=[
            pl.BlockSpec((rows, d), lambda l, j: (0, 0)),
            pl.BlockSpec((None, d, tn), lambda l, j: (l, 0, j)),
            pl.BlockSpec((None, 1, tn), lambda l, j: (l, 0, j)),
        ],
        out_specs=pl.BlockSpec((None, rows, tn), lambda l, j: (l, 0, j)),
        out_shape=jax.ShapeDtypeStruct((depth, rows, n), F32),
        compiler_params=_cparams(32, "parallel", "parallel"),
        name="modulation",
    )(cond, mod_w, mod_b.reshape(depth, 1, n))


def _scale_rows(y, a, b=None):
    tm, d = y.shape
    mr = a.shape[0]
    y = y.reshape(tm // mr, mr, d) * a[None]
    if b is not None:
        y = y + b[None]
    return y.reshape(tm, d)


def _inproj_kernel(x_ref, mod_ref, g_ref, w_ref, *out_refs, splits):
    x = x_ref[...]
    d = x.shape[1]
    y = x * lax.rsqrt(jnp.mean(x * x, axis=-1, keepdims=True) + NORM_EPS) * g_ref[...]
    m = mod_ref[...]
    n = _scale_rows(y, 1.0 + m[:, d:2 * d], m[:, :d]).astype(BF16)
    off = 0
    for o_ref, width in zip(out_refs, splits):
        o_ref[...] = jnp.dot(n, w_ref[:, off:off + width], preferred_element_type=F32).astype(o_ref.dtype)
        off += width


def _inproj(x, mod, mod_index, g, w, splits, dtypes, tm):
    r, d = x.shape
    mr = mod.shape[1]
    n = w.shape[1]
    return pl.pallas_call(
        functools.partial(_inproj_kernel, splits=splits),
        grid=(r // tm,),
        in_specs=[
            pl.BlockSpec((tm, d), lambda i: (i, 0)),
            pl.BlockSpec((None, mr, 3 * d), lambda i: (mod_index(i), 0, 0)),
            pl.BlockSpec((1, d), lambda i: (0, 0)),
            pl.BlockSpec((d, n), lambda i: (0, 0)),
        ],
        out_specs=[pl.BlockSpec((tm, s), lambda i: (i, 0)) for s in splits],
        out_shape=[jax.ShapeDtypeStruct((r, s), dt) for s, dt in zip(splits, dtypes)],
        compiler_params=_cparams(48, "parallel"),
        name="inproj",
    )(x, mod, g.reshape(1, d), w)


def _softplus(x):
    return jnp.log1p(jnp.exp(-jnp.abs(x))) + jnp.maximum(x, 0.0)


def _chunk_order(j, n_ctx, n_all):
    return jnp.where(j < n_ctx, n_ctx - 1 - j, n_all - 1 - (j - n_ctx))


def _lru_kernel(xa_ref, ga_ref, cw_ref, cb_ref, wg_ref, bg_ref, lam_ref, o_ref,
                hf_ref, a_ref, b_ref, *, n_ctx, n_all, ch):
    rows_total = xa_ref.shape[0]
    hd = xa_ref.shape[1]
    tb = ch // SUBLANE
    halo = 2 * SUBLANE
    row = lax.broadcasted_iota(jnp.int32, (ch, hd), 0)
    cw = cw_ref[...]
    cb = cb_ref[...]

    def conv(c):
        r0 = pl.multiple_of(c * ch, ch)
        first = jnp.logical_or(c == 0, c == n_ctx)
        last = jnp.logical_or(c == n_ctx - 1, c == n_all - 1)
        rp = pl.multiple_of(jnp.maximum(r0 - halo, 0), halo)
        rn = pl.multiple_of(jnp.minimum(r0 + ch, rows_total - halo), halo)
        ext = jnp.concatenate([xa_ref[pl.ds(rp, halo), :], xa_ref[pl.ds(r0, ch), :],
                               xa_ref[pl.ds(rn, halo), :]], axis=0).astype(F32)
        xm2 = jnp.where(jnp.logical_and(first, row < 2 * SUBLANE), 0.0, ext[0:ch])
        xm1 = jnp.where(jnp.logical_and(first, row < SUBLANE), 0.0, ext[SUBLANE:SUBLANE + ch])
        x0 = ext[2 * SUBLANE:2 * SUBLANE + ch]
        xp1 = jnp.where(jnp.logical_and(last, row >= ch - SUBLANE), 0.0, ext[3 * SUBLANE:3 * SUBLANE + ch])
        return cb + cw[0:1] * xm2 + cw[1:2] * xm1 + cw[2:3] * x0 + cw[3:4] * xp1

    def coeffs(u, d):
        pre = jnp.dot(u.astype(BF16), wg_ref[d], preferred_element_type=F32) + bg_ref[d]
        r = jax.nn.sigmoid(pre[:, :hd])
        i = jax.nn.sigmoid(pre[:, hd:])
        log_a = (-LRU_C) * r * _softplus(-lam_ref[d])
        a = jnp.exp(log_a)
        a_ref[...] = a
        b_ref[...] = jnp.sqrt(1.0 - a * a) * (i * u)

    def fwd_chunk(c, h):
        r0 = pl.multiple_of(c * ch, ch)
        coeffs(conv(c), 0)

        def step(t, h):
            rr = pl.multiple_of(t * SUBLANE, SUBLANE)
            h = a_ref[pl.ds(rr, SUBLANE), :] * h + b_ref[pl.ds(rr, SUBLANE), :]
            hf_ref[pl.ds(r0 + rr, SUBLANE), :] = h
            return h

        return lax.fori_loop(0, tb, step, h, unroll=True)

    def bwd_chunk(j, h):
        c = _chunk_order(j, n_ctx, n_all)
        r0 = pl.multiple_of(c * ch, ch)
        coeffs(conv(c), 1)

        def step(t, h):
            rr = pl.multiple_of((tb - 1 - t) * SUBLANE, SUBLANE)
            h = a_ref[pl.ds(rr, SUBLANE), :] * h + b_ref[pl.ds(rr, SUBLANE), :]
            b_ref[pl.ds(rr, SUBLANE), :] = h
            return h

        h = lax.fori_loop(0, tb, step, h, unroll=True)
        g = ga_ref[pl.ds(r0, ch), :].astype(F32)
        o_ref[pl.ds(r0, ch), :] = ((hf_ref[pl.ds(r0, ch), :] + b_ref[...]) * _silu(g)).astype(o_ref.dtype)
        return h

    h0 = jnp.zeros((SUBLANE, hd), F32)
    lax.fori_loop(0, n_all, fwd_chunk, h0)
    lax.fori_loop(0, n_all, bwd_chunk, h0)


def _lru_mix(xa, ga, conv_w, conv_b, wg, bg, lam, n_ctx_rows):
    r, w = xa.shape
    hd = LRU_HEAD_DIM
    ch = SCAN_STEPS * SUBLANE
    kern = functools.partial(_lru_kernel, n_ctx=n_ctx_rows // ch, n_all=r // ch, ch=ch)
    return pl.pallas_call(
        kern,
        grid=(w // hd,),
        in_specs=[
            pl.BlockSpec((r, hd), lambda h: (0, h)),
            pl.BlockSpec((r, hd), lambda h: (0, h)),
            pl.BlockSpec((conv_w.shape[0], hd), lambda h: (0, h)),
            pl.BlockSpec((1, hd), lambda h: (0, h)),
            pl.BlockSpec((2, None, hd, 2 * hd), lambda h: (0, h, 0, 0)),
            pl.BlockSpec((2, 1, 2 * hd), lambda h: (0, 0, h)),
            pl.BlockSpec((2, 1, hd), lambda h: (0, 0, h)),
        ],
        out_specs=pl.BlockSpec((r, hd), lambda h: (0, h)),
        out_shape=jax.ShapeDtypeStruct((r, w), BF16),
        scratch_shapes=[pltpu.VMEM((r, hd), F32), pltpu.VMEM((ch, hd), F32), pltpu.VMEM((ch, hd), F32)],
        compiler_params=_cparams(50, "parallel"),
        name="lru_mix",
    )(xa, ga, conv_w, conv_b, wg, bg, lam)


def _s5_disc_kernel(lre_ref, lim_ref, ldt_ref, bre_ref, bim_ref, are_ref, aim_ref, ore_ref, oim_ref):
    lam_re = lre_ref[...]
    lam_im = lim_ref[...]
    dt = jnp.exp(ldt_ref[...])
    mag = jnp.exp(lam_re * dt)
    ab_re = mag * jnp.cos(lam_im * dt)
    ab_im = mag * jnp.sin(lam_im * dt)
    den = lam_re * lam_re + lam_im * lam_im
    nr = ab_re - 1.0
    f_re = (nr * lam_re + ab_im * lam_im) / den
    f_im = (ab_im * lam_re - nr * lam_im) / den
    b_re = bre_ref[...]
    b_im = bim_ref[...]
    are_ref[...] = ab_re
    aim_ref[...] = ab_im
    ore_ref[...] = f_re * b_re - f_im * b_im
    oim_ref[...] = f_re * b_im + f_im * b_re


def _s5_discretise(lam_re, lam_im, log_dt, b_re, b_im):
    lead = b_re.shape[:-2]
    p, h = b_re.shape[-2:]
    nrow = math.prod(lead)
    flat = lambda v: jnp.broadcast_to(v[..., None], b_re.shape).reshape(nrow, p * h)
    shp = jax.ShapeDtypeStruct((nrow, p * h), F32)
    a_re, a_im, o_re, o_im = pl.pallas_call(
        _s5_disc_kernel, out_shape=[shp] * 4, name="s5_discretise",
    )(flat(lam_re), flat(lam_im), flat(log_dt), b_re.reshape(nrow, p * h), b_im.reshape(nrow, p * h))
    unflat = lambda v: v.reshape(*lead, p, h)
    return unflat(a_re)[..., 0], unflat(a_im)[..., 0], unflat(o_re), unflat(o_im)


def _s5_kernel(u_ref, bm_ref, cm_ref, ar_ref, ai_ref, dsk_ref, o_ref, yf_ref, v_ref, *, n_ctx, n_all, ch):
    ns = ar_ref.shape[-1]
    tb = ch // SUBLANE

    def run_chunk(c, carry, d):
        r0 = pl.multiple_of(c * ch, ch)
        v_ref[...] = jnp.dot(u_ref[pl.ds(r0, ch), :], bm_ref[d], preferred_element_type=F32)
        ar = jnp.broadcast_to(ar_ref[d], (SUBLANE, ns))
        ai = jnp.broadcast_to(ai_ref[d], (SUBLANE, ns))

        def step(t, hc):
            hr, hi = hc
            tt = t if d == 0 else tb - 1 - t
            rr = pl.multiple_of(tt * SUBLANE, SUBLANE)
            nr = ar * hr - ai * hi + v_ref[pl.ds(rr, SUBLANE), 0:ns]
            ni = ar * hi + ai * hr + v_ref[pl.ds(rr, SUBLANE), ns:2 * ns]
            v_ref[pl.ds(rr, SUBLANE), 0:ns] = nr
            v_ref[pl.ds(rr, SUBLANE), ns:2 * ns] = ni
            return nr, ni

        carry = lax.fori_loop(0, tb, step, carry, unroll=8)
        y = jnp.dot(v_ref[...].astype(BF16), cm_ref[d], preferred_element_type=F32)
        return r0, y, carry

    def fwd_chunk(c, carry):
        r0, y, carry = run_chunk(c, carry, 0)
        yf_ref[pl.ds(r0, ch), :] = y
        return carry

    def bwd_chunk(j, carry):
        r0, y, carry = run_chunk(_chunk_order(j, n_ctx, n_all), carry, 1)
        u = u_ref[pl.ds(r0, ch), :].astype(F32)
        y = yf_ref[pl.ds(r0, ch), :] + y + dsk_ref[...] * u
        o_ref[pl.ds(r0, ch), :] = jax.nn.gelu(y).astype(o_ref.dtype)
        return carry

    z = jnp.zeros((SUBLANE, ns), F32)
    lax.fori_loop(0, n_all, fwd_chunk, (z, z))
    lax.fori_loop(0, n_all, bwd_chunk, (z, z))


def _s5_mix(ub, bm, cm, ar, ai, dsk, n_ctx_rows):
    r, w = ub.shape
    ns = ar.shape[-1]
    ch = SCAN_STEPS * SUBLANE
    kern = functools.partial(_s5_kernel, n_ctx=n_ctx_rows // ch, n_all=r // ch, ch=ch)
    return pl.pallas_call(
        kern,
        grid=(w // LANE,),
        in_specs=[
            pl.BlockSpec((r, LANE), lambda s: (0, s)),
            pl.BlockSpec((2, None, LANE, 2 * ns), lambda s: (0, s, 0, 0)),
            pl.BlockSpec((2, None, 2 * ns, LANE), lambda s: (0, s, 0, 0)),
            pl.BlockSpec((2, None, 1, ns), lambda s: (0, s, 0, 0)),
            pl.BlockSpec((2, None, 1, ns), lambda s: (0, s, 0, 0)),
            pl.BlockSpec((1, LANE), lambda s: (0, s)),
        ],
        out_specs=pl.BlockSpec((r, LANE), lambda s: (0, s)),
        out_shape=jax.ShapeDtypeStruct((r, w), BF16),
        scratch_shapes=[pltpu.VMEM((r, LANE), F32), pltpu.VMEM((ch, 2 * ns), F32)],
        compiler_params=_cparams(44, "parallel"),
        name="s5_mix",
    )(ub, bm, cm, ar, ai, dsk)


def _outproj_even_kernel(ya_ref, yg_ref, gb_ref, x_ref, mod_ref, gw_ref, gbias_ref, w1_ref, w2_ref, o_ref):
    d = x_ref.shape[1]
    yg = yg_ref[...]
    z = jnp.dot(yg, gw_ref[...], preferred_element_type=F32) + gbias_ref[...]
    yb = yg.astype(F32) * jax.nn.sigmoid(z) * _silu(gb_ref[...].astype(F32))
    o = jnp.dot(ya_ref[...], w1_ref[...], preferred_element_type=F32)
    o = o + jnp.dot(yb.astype(BF16), w2_ref[...], preferred_element_type=F32)
    gate = mod_ref[...][:, 2 * d:]
    o_ref[...] = x_ref[...] + _scale_rows(o, gate)


def _outproj_even(ya, yg, gb, x, mod, mod_index, glu_w, glu_b, w1, w2, tm):
    r, d = x.shape
    wa, wb = ya.shape[1], yg.shape[1]
    mr = mod.shape[1]
    rows = lambda n: pl.BlockSpec((tm, n), lambda i: (i, 0))
    full = lambda a: pl.BlockSpec(a.shape, lambda i: (0, 0))
    return pl.pallas_call(
        _outproj_even_kernel,
        grid=(r // tm,),
        in_specs=[rows(wa), rows(wb), rows(wb), rows(d),
                  pl.BlockSpec((None, mr, 3 * d), lambda i: (mod_index(i), 0, 0)),
                  full(glu_w), full(glu_b), full(w1), full(w2)],
        out_specs=rows(d),
        out_shape=jax.ShapeDtypeStruct((r, d), F32),
        compiler_params=_cparams(40, "parallel"),
        name="outproj_even",
    )(ya, yg, gb, x, mod, glu_w, glu_b, w1, w2)


def _outproj_odd_kernel(o_ref, x_ref, mod_ref, w_ref, *rest, final):
    d = x_ref.shape[1]
    y = jnp.dot(o_ref[...], w_ref[...], preferred_element_type=F32)
    x = x_ref[...] + _scale_rows(y, mod_ref[...][:, 2 * d:])
    if final:
        g_ref, out_ref = rest
        x = x * lax.rsqrt(jnp.mean(x * x, axis=-1, keepdims=True) + NORM_EPS) * g_ref[...]
    else:
        (out_ref,) = rest
    out_ref[...] = x


def _outproj_odd(o, x, mod, w, tm, blocks_per_sample):
    r, d = x.shape
    mod_index = lambda i: jnp.where(i % blocks_per_sample == 0, mod.shape[0] // 2, 0) + i // blocks_per_sample
    return pl.pallas_call(
        functools.partial(_outproj_odd_kernel, final=False),
        grid=(r // tm,),
        in_specs=[pl.BlockSpec((tm, o.shape[1]), lambda i: (i, 0)),
                  pl.BlockSpec((tm, d), lambda i: (i, 0)),
                  pl.BlockSpec((None, 1, 3 * d), lambda i: (mod_index(i), 0, 0)),
                  pl.BlockSpec(w.shape, lambda i: (0, 0))],
        out_specs=pl.BlockSpec((tm, d), lambda i: (i, 0)),
        out_shape=jax.ShapeDtypeStruct((r, d), F32),
        compiler_params=_cparams(32, "parallel"),
        name="outproj_odd",
    )(o, x, mod, w)


def _outproj_final(o, x, mod, w, final_g, tm, bsz, blocks_per_sample):
    r, d = x.shape
    nlat = blocks_per_sample - 1
    rblk = lambda b, j: (b * blocks_per_sample + 1 + j, 0)
    return pl.pallas_call(
        functools.partial(_outproj_odd_kernel, final=True),
        grid=(bsz, nlat),
        in_specs=[pl.BlockSpec((tm, o.shape[1]), rblk),
                  pl.BlockSpec((tm, d), rblk),
                  pl.BlockSpec((None, 1, 3 * d), lambda b, j: (b, 0, 0)),
                  pl.BlockSpec(w.shape, lambda b, j: (0, 0)),
                  pl.BlockSpec((1, d), lambda b, j: (0, 0))],
        out_specs=pl.BlockSpec((None, tm, d), lambda b, j: (b, j, 0)),
        out_shape=jax.ShapeDtypeStruct((bsz, nlat * tm, d), F32),
        compiler_params=_cparams(32, "parallel", "parallel"),
        name="outproj_final",
    )(o, x, mod, w, final_g.reshape(1, d))


def _rms(x, g):
    return x * lax.rsqrt(jnp.mean(x * x, axis=-1, keepdims=True) + NORM_EPS) * g


def _rope_fold(v, tab):
    w = v * tab
    return w + pltpu.roll(w, LANE // 2, axis=1)


def _mla_up_kernel(cq_ref, ckv_ref, kr_ref, tab_ref, qg_ref, kvg_ref, wq_ref, wkv_ref, q_ref, k_ref, v_ref):
    heads = q_ref.shape[0]
    tab = tab_ref[...]
    q = jnp.dot(_rms(cq_ref[...], qg_ref[...]).astype(BF16), wq_ref[...], preferred_element_type=F32)
    kv = jnp.dot(_rms(ckv_ref[...], kvg_ref[...]).astype(BF16), wkv_ref[...], preferred_element_type=F32)
    kr = _rope_fold(kr_ref[...], tab)[:, :MLA_ROPE].astype(BF16)
    per = MLA_NOPE + 2 * MLA_ROPE
    for h in range(heads):
        qh = q[:, h * per:(h + 1) * per]
        q_ref[h, :, 0:MLA_NOPE] = (qh[:, :MLA_NOPE] * MLA_SCALE).astype(BF16)
        qr = _rope_fold(qh[:, MLA_NOPE:], tab)[:, :MLA_ROPE] * MLA_SCALE
        q_ref[h, :, MLA_NOPE:MLA_NOPE + MLA_ROPE] = qr.astype(BF16)
        kvh = kv[:, h * (MLA_NOPE + MLA_V):(h + 1) * (MLA_NOPE + MLA_V)]
        k_ref[h, :, 0:MLA_NOPE] = kvh[:, :MLA_NOPE].astype(BF16)
        k_ref[h, :, MLA_NOPE:MLA_NOPE + MLA_ROPE] = kr
        v_ref[h] = kvh[:, MLA_NOPE:].astype(BF16)


def _mla_up(cq, ckv, krr, tab, q_norm, kv_norm, wq, wkv, tm, blocks_per_sample):
    r = cq.shape[0]
    heads = MLA_HEADS
    dk = MLA_NOPE + MLA_ROPE
    rows = lambda n: pl.BlockSpec((tm, n), lambda i: (i, 0))
    full = lambda a: pl.BlockSpec(a.shape, lambda i: (0, 0))
    hrows = lambda n: pl.BlockSpec((heads, tm, n), lambda i: (0, i, 0))
    return pl.pallas_call(
        _mla_up_kernel,
        grid=(r // tm,),
        in_specs=[rows(cq.shape[1]), rows(ckv.shape[1]), rows(krr.shape[1]),
                  pl.BlockSpec((tm, LANE), lambda i: (i % blocks_per_sample, 0)),
                  full(q_norm), full(kv_norm), full(wq), full(wkv)],
        out_specs=[hrows(dk), hrows(dk), hrows(MLA_V)],
        out_shape=[jax.ShapeDtypeStruct((heads, r, dk), BF16), jax.ShapeDtypeStruct((heads, r, dk), BF16),
                   jax.ShapeDtypeStruct((heads, r, MLA_V), BF16)],
        compiler_params=_cparams(40, "parallel"),
        name="mla_up",
    )(cq, ckv, krr, tab, q_norm, kv_norm, wq, wkv)


def _attn_kernel(q_ref, k_ref, v_ref, g_ref, o_ref, *, q_blk0, n_ctx):
    def attend(nk):
        s = lax.dot_general(q_ref[...], k_ref[0:nk, :], (((1,), (1,)), ((), ())), preferred_element_type=F32)
        p = jnp.exp(s - jnp.max(s, axis=-1, keepdims=True))
        l = jnp.sum(p, axis=-1, keepdims=True)
        o = jnp.dot(p.astype(BF16), v_ref[0:nk, :], preferred_element_type=F32) / l
        o_ref[...] = (o * _silu(g_ref[...].astype(F32))).astype(o_ref.dtype)

    if q_blk0 == 0:
        is_ctx = pl.program_id(2) == 0
        pl.when(is_ctx)(lambda: attend(n_ctx))
        pl.when(jnp.logical_not(is_ctx))(lambda: attend(k_ref.shape[0]))
    else:
        attend(k_ref.shape[0])


def _attention(q, k, v, gate, bsz, tq, n_ctx, with_ctx):
    heads, r, dk = q.shape
    per = r // bsz
    nqb = per // tq
    q_blk0 = 0 if with_ctx else n_ctx // tq
    rblk = lambda b, j: b * nqb + q_blk0 + j
    return pl.pallas_call(
        functools.partial(_attn_kernel, q_blk0=q_blk0, n_ctx=n_ctx),
        grid=(bsz, heads, nqb - q_blk0),
        in_specs=[pl.BlockSpec((None, tq, dk), lambda b, h, j: (h, rblk(b, j), 0)),
                  pl.BlockSpec((None, per, dk), lambda b, h, j: (h, b, 0)),
                  pl.BlockSpec((None, per, MLA_V), lambda b, h, j: (h, b, 0)),
                  pl.BlockSpec((tq, MLA_V), lambda b, h, j: (rblk(b, j), h))],
        out_specs=pl.BlockSpec((tq, MLA_V), lambda b, h, j: (rblk(b, j), h)),
        out_shape=jax.ShapeDtypeStruct((r, heads * MLA_V), BF16),
        compiler_params=_cparams(40, "parallel", "parallel", "parallel"),
        name="mla_attention",
    )(q, k, v, gate)


def _s5_matrices(a_re, a_im, bb_re, bb_im, c_re, c_im):
    gps = LANE // S5_GROUP
    nd, g, p, h = bb_re.shape
    slabs = g // gps
    eye = jnp.eye(gps, dtype=F32)
    bd_in = lambda m: jnp.einsum("dsgpj,gk->dsgjkp", m.reshape(nd, slabs, gps, p, h), eye).reshape(nd, slabs, gps * h, gps * p)
    bd_out = lambda m: jnp.einsum("dsgip,gk->dskpgi", m.reshape(nd, slabs, gps, h, p), eye).reshape(nd, slabs, gps * p, gps * h)
    bm = jnp.concatenate([bd_in(bb_re), bd_in(bb_im)], axis=-1).astype(BF16)
    cm = jnp.concatenate([bd_out(c_re), -bd_out(c_im)], axis=-2).astype(BF16)
    ar = a_re.reshape(nd, slabs, 1, gps * p)
    ai = a_im.reshape(nd, slabs, 1, gps * p)
    return bm, cm, ar, ai


def _rope_table(n_ctx, n_lat):
    axis = MLA_ROPE // 2
    rows = n_lat // GRID_W
    row = jnp.repeat(jnp.arange(rows, dtype=F32), GRID_W)
    col = jnp.tile(jnp.arange(GRID_W, dtype=F32), rows)
    inv = ROPE_BASE ** (-jnp.arange(0, axis, 2, dtype=F32) / axis)
    ang = jnp.concatenate([row[:, None] * inv, col[:, None] * inv], axis=-1)
    cos, sin = jnp.cos(ang), jnp.sin(ang)
    lat = jnp.concatenate([cos, cos, -sin, sin], axis=-1)
    one, zero = jnp.ones((n_ctx, axis), F32), jnp.zeros((n_ctx, axis), F32)
    return jnp.concatenate([jnp.concatenate([one, one, zero, zero], axis=-1), lat], axis=0)


def _swap_halves(w):
    half = w.shape[-1] // 2
    return jnp.concatenate([w[..., half:], w[..., :half]], axis=-1)


def kernel(x, c, ctx, c_ctx, norm_g, mod_w, mod_b, ev_w_in, lru_conv_w, lru_conv_b, lru_wr, lru_br, lru_wi, lru_bi, lru_lam, s5_lam_re, s5_lam_im, s5_log_dt, s5_b_re, s5_b_im, s5_c_re, s5_c_im, s5_d, s5_glu_w, s5_glu_b, ev_w_out, mla_w_in, mla_q_norm, mla_w_uq, mla_kv_norm, mla_w_ukv, mla_w_out, final_g):
    bsz, n_lat, d = x.shape
    n_ctx = ctx.shape[1]
    steps = n_ctx + n_lat
    depth = mod_w.shape[0]
    assert bsz == SUBLANE, "time-major layout maps the batch onto the sublanes of one vreg"
    lru_w = lru_conv_w.shape[-1]
    s5_w = s5_d.shape[-2] * s5_d.shape[-1]
    heads = MLA_HEADS
    q_rank, kv_rank = mla_q_norm.shape[-1], mla_kv_norm.shape[-1]

    to_tm = lambda v: v.reshape(bsz, steps, d).transpose(1, 0, 2).reshape(steps * bsz, d)
    to_bm = lambda v: v.reshape(steps, bsz, d).transpose(1, 0, 2).reshape(bsz * steps, d)

    cond = jnp.concatenate([c, jnp.broadcast_to(c_ctx[None], (bsz, d))], axis=0)
    mod = _modulation(cond, mod_w, mod_b)
    mod_tm = mod.reshape(depth, 2, bsz, 3 * d)
    mod_bm = mod.reshape(depth, 2 * bsz, 1, 3 * d)

    a_re, a_im, bb_re, bb_im = _s5_discretise(s5_lam_re, s5_lam_im, s5_log_dt, s5_b_re, s5_b_im)
    rope_tab = _rope_table(n_ctx, n_lat)

    tm_rows = 512
    ctx_blocks_tm = n_ctx * bsz // tm_rows
    tm_index = lambda i: jnp.where(i < ctx_blocks_tm, 1, 0)
    tq = n_ctx
    blocks_per_sample = steps // tq
    bm_index = lambda i: jnp.where(i % blocks_per_sample == 0, bsz, 0) + i // blocks_per_sample

    xs = jnp.concatenate([ctx, x], axis=1).reshape(bsz * steps, d)
    out = None
    for l in range(depth):
        if l % 2 == 0:
            e = l // 2
            xt = to_tm(xs)
            xa, ga, ub, gb = _inproj(xt, mod_tm[l], tm_index, norm_g[l], ev_w_in[e].astype(BF16),
                                     (lru_w, lru_w, s5_w, s5_w), (BF16,) * 4, tm_rows)
            nh = lru_w // LRU_HEAD_DIM
            wg = jnp.concatenate([lru_wr[e], lru_wi[e]], axis=-1).astype(BF16)
            bg = jnp.concatenate([lru_br[e].reshape(2, nh, LRU_HEAD_DIM), lru_bi[e].reshape(2, nh, LRU_HEAD_DIM)],
                                 axis=-1).reshape(2, 1, 2 * lru_w)
            ya = _lru_mix(xa, ga, lru_conv_w[e], lru_conv_b[e].reshape(1, lru_w), wg, bg,
                          lru_lam[e].reshape(2, 1, lru_w), n_ctx * bsz)
            bm, cm, ar, ai = _s5_matrices(a_re[e], a_im[e], bb_re[e], bb_im[e], s5_c_re[e], s5_c_im[e])
            yg = _s5_mix(ub, bm, cm, ar, ai, s5_d[e].reshape(1, s5_w), n_ctx * bsz)
            w_out = ev_w_out[e].astype(BF16)
            xt = _outproj_even(ya, yg, gb, xt, mod_tm[l], tm_index, s5_glu_w[e].astype(BF16),
                               s5_glu_b[e].reshape(1, s5_w), w_out[:lru_w], w_out[lru_w:], tm_rows)
            xs = to_bm(xt)
        else:
            o = l // 2
            with_ctx = l < depth - 1
            w_in = mla_w_in[o]
            off_kr, off_g = q_rank + kv_rank, q_rank + kv_rank + MLA_ROPE
            w_kr = w_in[:, off_kr:off_g]
            w_in = jnp.concatenate([w_in[:, :off_kr], w_in[:, off_g:], w_kr, _swap_halves(w_kr)], axis=-1).astype(BF16)
            gate_w = heads * MLA_V
            cq, ckv, gate, krr = _inproj(xs, mod_bm[l], bm_index, norm_g[l], w_in,
                                         (q_rank, kv_rank, gate_w, 2 * MLA_ROPE), (F32, F32, BF16, F32), tq)
            wq = mla_w_uq[o].reshape(q_rank, heads, MLA_NOPE + MLA_ROPE)
            wq_r = wq[..., MLA_NOPE:]
            wq = jnp.concatenate([wq, _swap_halves(wq_r)], axis=-1).reshape(q_rank, -1).astype(BF16)
            q, k, v = _mla_up(cq, ckv, krr, rope_tab, mla_q_norm[o].reshape(1, q_rank),
                              mla_kv_norm[o].reshape(1, kv_rank), wq, mla_w_ukv[o].astype(BF16), tq, blocks_per_sample)
            att = _attention(q, k, v, gate, bsz, tq, n_ctx, with_ctx)
            w_out = mla_w_out[o].astype(BF16)
            if with_ctx:
                xs = _outproj_odd(att, xs, mod_bm[l], w_out, tq, blocks_per_sample)
            else:
                out = _outproj_final(att, xs, mod_bm[l], w_out, final_g, tq, bsz, blocks_per_sample)
    return out
```

```python
import functools
import math

import jax
import jax.numpy as jnp
from jax import lax
from jax.experimental import pallas as pl
from jax.experimental.pallas import tpu as pltpu

F32 = jnp.float32
BF16 = jnp.bfloat16

NORM_EPS = 1e-6
LRU_C = 8.0
LRU_HEAD_DIM = 128
S5_GROUP = 16
S5_STATE = 64
MLA_HEADS = 8
MLA_NOPE = 128
MLA_ROPE = 64
MLA_V = 128
MLA_SCALE = 1.0 / math.sqrt(MLA_NOPE + MLA_ROPE)
Q_PRESCALE = MLA_SCALE * math.log2(math.e)
ROPE_BASE = 10000.0
GRID_W = 64

LANE = 128
SUBLANE = 8
SCAN_STEPS = 32
MIB = 1024 * 1024


def _cparams(vmem_mib, *sem):
    return pltpu.CompilerParams(dimension_semantics=tuple(sem), vmem_limit_bytes=int(vmem_mib * MIB))


def _sigmoid(x):
    return 0.5 * jnp.tanh(0.5 * x) + 0.5


def _silu(x):
    return x * _sigmoid(x)


def _sqrt_nonneg(x):
    return jnp.where(x > 0.0, x * lax.rsqrt(x), 0.0)


def _mod_kernel(c_ref, w_ref, b_ref, o_ref):
    a = _silu(c_ref[...])
    o_ref[...] = jnp.dot(a, w_ref[...], preferred_element_type=F32, precision=lax.Precision.HIGHEST) + b_ref[...]


def _modulation(cond, mod_w, mod_b):
    depth, d, n = mod_w.shape
    rows = cond.shape[0]
    tn = min(n, 1024)
    return pl.pallas_call(
        _mod_kernel,
        grid=(depth, n // tn),
        in_specs=[
            pl.BlockSpec((rows, d), lambda l, j: (0, 0)),
            pl.BlockSpec((None, d, tn), lambda l, j: (l, 0, j)),
            pl.BlockSpec((None, 1, tn), lambda l, j: (l, 0, j)),
        ],
        out_specs=pl.BlockSpec((None, rows, tn), lambda l, j: (l, 0, j)),
        out_shape=jax.ShapeDtypeStruct((depth, rows, n), F32),
        compiler_params=_cparams(32, "parallel", "parallel"),
        name="modulation",
    )(cond, mod_w, mod_b.reshape(depth, 1, n))


def _scale_rows(y, a, b=None):
    tm, d = y.shape
    mr = a.shape[0]
    if mr == 1:
        return y * a if b is None else y * a + b
    y = y.reshape(tm // mr, mr, d) * a[None]
    if b is not None:
        y = y + b[None]
    return y.reshape(tm, d)


def _inproj_kernel(x_ref, mod_ref, g_ref, w_ref, *out_refs, splits):
    x = x_ref[...]
    d = x.shape[1]
    y = x * lax.rsqrt(jnp.mean(x * x, axis=-1, keepdims=True) + NORM_EPS) * g_ref[...]
    m = mod_ref[...]
    n = _scale_rows(y, 1.0 + m[:, d:2 * d], m[:, :d]).astype(BF16)
    off = 0
    for o_ref, width in zip(out_refs, splits):
        o_ref[...] = jnp.dot(n, w_ref[:, off:off + width], preferred_element_type=F32).astype(o_ref.dtype)
        off += width


def _inproj(x, mod, mod_index, g, w, splits, dtypes, tm):
    r, d = x.shape
    mr = mod.shape[1]
    n = w.shape[1]
    return pl.pallas_call(
        functools.partial(_inproj_kernel, splits=splits),
        grid=(r // tm,),
        in_specs=[
            pl.BlockSpec((tm, d), lambda i: (i, 0)),
            pl.BlockSpec((None, mr, 3 * d), lambda i: (mod_index(i), 0, 0)),
            pl.BlockSpec((1, d), lambda i: (0, 0)),
            pl.BlockSpec((d, n), lambda i: (0, 0)),
        ],
        out_specs=[pl.BlockSpec((tm, s), lambda i: (i, 0)) for s in splits],
        out_shape=[jax.ShapeDtypeStruct((r, s), dt) for s, dt in zip(splits, dtypes)],
        compiler_params=_cparams(48, "parallel"),
        name="inproj",
    )(x, mod, g.reshape(1, d), w)


def _softplus(x):
    return jnp.log1p(jnp.exp(-jnp.abs(x))) + jnp.maximum(x, 0.0)


def _chunk_order(j, n_ctx, n_all):
    return jnp.where(j < n_ctx, n_ctx - 1 - j, n_all - 1 - (j - n_ctx))


def _lru_kernel(xa_ref, ga_ref, cw_ref, cb_ref, wg_ref, bg_ref, lam_ref, o_ref,
                hf_ref, hb_ref, *ab_refs, n_ctx, n_all, ch):
    rows_total = xa_ref.shape[0]
    hd = xa_ref.shape[1]
    tb = ch // SUBLANE
    halo = 2 * SUBLANE
    row = lax.broadcasted_iota(jnp.int32, (ch, hd), 0)
    cw = cw_ref[...]
    cb = cb_ref[...]
    bufs = ((ab_refs[0:2], ab_refs[2:4]), (ab_refs[4:6], ab_refs[6:8]))
    dst = (hf_ref, hb_ref)
    order = (lambda j: j, lambda j: _chunk_order(j, n_ctx, n_all))
    neg_c_softplus = [(-LRU_C) * _softplus(-lam_ref[d]) for d in (0, 1)]

    def conv(c):
        r0 = pl.multiple_of(c * ch, ch)
        first = jnp.logical_or(c == 0, c == n_ctx)
        last = jnp.logical_or(c == n_ctx - 1, c == n_all - 1)
        rp = pl.multiple_of(jnp.maximum(r0 - halo, 0), halo)
        rn = pl.multiple_of(jnp.minimum(r0 + ch, rows_total - halo), halo)
        ext = jnp.concatenate([xa_ref[pl.ds(rp, halo), :], xa_ref[pl.ds(r0, ch), :],
                               xa_ref[pl.ds(rn, halo), :]], axis=0).astype(F32)
        xm2 = jnp.where(jnp.logical_and(first, row < 2 * SUBLANE), 0.0, ext[0:ch])
        xm1 = jnp.where(jnp.logical_and(first, row < SUBLANE), 0.0, ext[SUBLANE:SUBLANE + ch])
        x0 = ext[2 * SUBLANE:2 * SUBLANE + ch]
        xp1 = jnp.where(jnp.logical_and(last, row >= ch - SUBLANE), 0.0, ext[3 * SUBLANE:3 * SUBLANE + ch])
        return cb + cw[0:1] * xm2 + cw[1:2] * xm1 + cw[2:3] * x0 + cw[3:4] * xp1

    def coeffs(c, d, a_ref, b_ref):
        u = conv(c)
        pre = jnp.dot(u.astype(BF16), wg_ref[d], preferred_element_type=F32) + bg_ref[d]
        a = jnp.exp(_sigmoid(pre[:, :hd]) * neg_c_softplus[d])
        a_ref[...] = a
        b_ref[...] = _sqrt_nonneg(1.0 - a * a) * (_sigmoid(pre[:, hd:]) * u)

    def scan(c, d, a_ref, b_ref, h):
        r0 = pl.multiple_of(c * ch, ch)
        for t in range(tb):
            rr = (t if d == 0 else tb - 1 - t) * SUBLANE
            h = a_ref[rr:rr + SUBLANE, :] * h + b_ref[rr:rr + SUBLANE, :]
            dst[d][pl.ds(r0 + rr, SUBLANE), :] = h
        return h

    for d in (0, 1):
        coeffs(order[d](0), d, *bufs[d][0])

    def chunk_pair(i, hs):
        hs = list(hs)
        for slot in (0, 1):
            j = 2 * i + slot
            nxt = jnp.minimum(j + 1, n_all - 1)
            for d in (0, 1):
                coeffs(order[d](nxt), d, *bufs[d][1 - slot])
            for d in (0, 1):
                hs[d] = scan(order[d](j), d, *bufs[d][slot], hs[d])
        return tuple(hs)

    h0 = jnp.zeros((SUBLANE, hd), F32)
    lax.fori_loop(0, n_all // 2, chunk_pair, (h0, h0))

    def finish(c, carry):
        r0 = pl.multiple_of(c * ch, ch)
        g = ga_ref[pl.ds(r0, ch), :].astype(F32)
        o_ref[pl.ds(r0, ch), :] = ((hf_ref[pl.ds(r0, ch), :] + hb_ref[pl.ds(r0, ch), :]) * _silu(g)).astype(o_ref.dtype)
        return carry

    lax.fori_loop(0, n_all, finish, 0)


def _lru_mix(xa, ga, conv_w, conv_b, wg, bg, lam, n_ctx_rows):
    r, w = xa.shape
    hd = LRU_HEAD_DIM
    ch = SCAN_STEPS * SUBLANE
    assert (r // ch) % 2 == 0, "chunks are processed in pairs"
    kern = functools.partial(_lru_kernel, n_ctx=n_ctx_rows // ch, n_all=r // ch, ch=ch)
    return pl.pallas_call(
        kern,
        grid=(w // hd,),
        in_specs=[
            pl.BlockSpec((r, hd), lambda h: (0, h)),
            pl.BlockSpec((r, hd), lambda h: (0, h)),
            pl.BlockSpec((conv_w.shape[0], hd), lambda h: (0, h)),
            pl.BlockSpec((1, hd), lambda h: (0, h)),
            pl.BlockSpec((2, None, hd, 2 * hd), lambda h: (0, h, 0, 0)),
            pl.BlockSpec((2, 1, 2 * hd), lambda h: (0, 0, h)),
            pl.BlockSpec((2, 1, hd), lambda h: (0, 0, h)),
        ],
        out_specs=pl.BlockSpec((r, hd), lambda h: (0, h)),
        out_shape=jax.ShapeDtypeStruct((r, w), BF16),
        scratch_shapes=[pltpu.VMEM((r, hd), F32)] * 2 + [pltpu.VMEM((ch, hd), F32)] * 8,
        compiler_params=_cparams(56, "parallel"),
        name="lru_mix",
    )(xa, ga, conv_w, conv_b, wg, bg, lam)


def _s5_disc_kernel(lre_ref, lim_ref, ldt_ref, bre_ref, bim_ref, are_ref, aim_ref, ore_ref, oim_ref):
    lam_re = lre_ref[...]
    lam_im = lim_ref[...]
    dt = jnp.exp(ldt_ref[...])
    mag = jnp.exp(lam_re * dt)
    ab_re = mag * jnp.cos(lam_im * dt)
    ab_im = mag * jnp.sin(lam_im * dt)
    den = lam_re * lam_re + lam_im * lam_im
    nr = ab_re - 1.0
    f_re = (nr * lam_re + ab_im * lam_im) / den
    f_im = (ab_im * lam_re - nr * lam_im) / den
    b_re = bre_ref[...]
    b_im = bim_ref[...]
    are_ref[...] = ab_re
    aim_ref[...] = ab_im
    ore_ref[...] = f_re * b_re - f_im * b_im
    oim_ref[...] = f_re * b_im + f_im * b_re


def _s5_discretise(lam_re, lam_im, log_dt, b_re, b_im):
    lead = b_re.shape[:-2]
    p, h = b_re.shape[-2:]
    nrow = math.prod(lead)
    flat = lambda v: jnp.broadcast_to(v[..., None], b_re.shape).reshape(nrow, p * h)
    shp = jax.ShapeDtypeStruct((nrow, p * h), F32)
    a_re, a_im, o_re, o_im = pl.pallas_call(
        _s5_disc_kernel, out_shape=[shp] * 4, name="s5_discretise",
    )(flat(lam_re), flat(lam_im), flat(log_dt), b_re.reshape(nrow, p * h), b_im.reshape(nrow, p * h))
    unflat = lambda v: v.reshape(*lead, p, h)
    return unflat(a_re)[..., 0], unflat(a_im)[..., 0], unflat(o_re), unflat(o_im)


def _s5_kernel(u_ref, bm_ref, cm_ref, ar_ref, ai_ref, dsk_ref, o_ref, yf_ref, yb_ref, vf_ref, vb_ref,
               *, n_ctx, n_all, ch):
    ns = ar_ref.shape[-1]
    tb = ch // SUBLANE
    v_refs = (vf_ref, vb_ref)
    y_refs = (yf_ref, yb_ref)
    order = (lambda j: j, lambda j: _chunk_order(j, n_ctx, n_all))

    def chunk(j, carry):
        hs = [list(carry[0:2]), list(carry[2:4])]
        r0 = [pl.multiple_of(order[d](j) * ch, ch) for d in (0, 1)]
        for d in (0, 1):
            v_refs[d][...] = jnp.dot(u_ref[pl.ds(r0[d], ch), :], bm_ref[d], preferred_element_type=F32)
        ar = [jnp.broadcast_to(ar_ref[d], (SUBLANE, ns)) for d in (0, 1)]
        ai = [jnp.broadcast_to(ai_ref[d], (SUBLANE, ns)) for d in (0, 1)]
        for t in range(tb):
            for d in (0, 1):
                rr = (t if d == 0 else tb - 1 - t) * SUBLANE
                hr, hi = hs[d]
                nr = ar[d] * hr - ai[d] * hi + v_refs[d][rr:rr + SUBLANE, 0:ns]
                ni = ar[d] * hi + ai[d] * hr + v_refs[d][rr:rr + SUBLANE, ns:2 * ns]
                v_refs[d][rr:rr + SUBLANE, 0:ns] = nr
                v_refs[d][rr:rr + SUBLANE, ns:2 * ns] = ni
                hs[d] = [nr, ni]
        for d in (0, 1):
            y_refs[d][pl.ds(r0[d], ch), :] = jnp.dot(v_refs[d][...].astype(BF16), cm_ref[d], preferred_element_type=F32)
        return (*hs[0], *hs[1])

    z = jnp.zeros((SUBLANE, ns), F32)
    lax.fori_loop(0, n_all, chunk, (z, z, z, z))

    def finish(c, carry):
        r0 = pl.multiple_of(c * ch, ch)
        u = u_ref[pl.ds(r0, ch), :].astype(F32)
        y = yf_ref[pl.ds(r0, ch), :] + yb_ref[pl.ds(r0, ch), :] + dsk_ref[...] * u
        o_ref[pl.ds(r0, ch), :] = jax.nn.gelu(y).astype(o_ref.dtype)
        return carry

    lax.fori_loop(0, n_all, finish, 0)


def _s5_mix(ub, bm, cm, ar, ai, dsk, n_ctx_rows):
    r, w = ub.shape
    ns = ar.shape[-1]
    ch = SCAN_STEPS * SUBLANE
    kern = functools.partial(_s5_kernel, n_ctx=n_ctx_rows // ch, n_all=r // ch, ch=ch)
    return pl.pallas_call(
        kern,
        grid=(w // LANE,),
        in_specs=[
            pl.BlockSpec((r, LANE), lambda s: (0, s)),
            pl.BlockSpec((2, None, LANE, 2 * ns), lambda s: (0, s, 0, 0)),
            pl.BlockSpec((2, None, 2 * ns, LANE), lambda s: (0, s, 0, 0)),
            pl.BlockSpec((2, None, 1, ns), lambda s: (0, s, 0, 0)),
            pl.BlockSpec((2, None, 1, ns), lambda s: (0, s, 0, 0)),
            pl.BlockSpec((1, LANE), lambda s: (0, s)),
        ],
        out_specs=pl.BlockSpec((r, LANE), lambda s: (0, s)),
        out_shape=jax.ShapeDtypeStruct((r, w), BF16),
        scratch_shapes=[pltpu.VMEM((r, LANE), F32)] * 2 + [pltpu.VMEM((ch, 2 * ns), F32)] * 2,
        compiler_params=_cparams(52, "parallel"),
        name="s5_mix",
    )(ub, bm, cm, ar, ai, dsk)


def _outproj_even_kernel(ya_ref, yg_ref, gb_ref, x_ref, mod_ref, gw_ref, gbias_ref, w1_ref, w2_ref, o_ref):
    d = x_ref.shape[1]
    yg = yg_ref[...]
    z = jnp.dot(yg, gw_ref[...], preferred_element_type=F32) + gbias_ref[...]
    yb = yg.astype(F32) * _sigmoid(z) * _silu(gb_ref[...].astype(F32))
    o = jnp.dot(ya_ref[...], w1_ref[...], preferred_element_type=F32)
    o = o + jnp.dot(yb.astype(BF16), w2_ref[...], preferred_element_type=F32)
    gate = mod_ref[...][:, 2 * d:]
    o_ref[...] = x_ref[...] + _scale_rows(o, gate)


def _outproj_even(ya, yg, gb, x, mod, mod_index, glu_w, glu_b, w1, w2, tm):
    r, d = x.shape
    wa, wb = ya.shape[1], yg.shape[1]
    mr = mod.shape[1]
    rows = lambda n: pl.BlockSpec((tm, n), lambda i: (i, 0))
    full = lambda a: pl.BlockSpec(a.shape, lambda i: (0, 0))
    return pl.pallas_call(
        _outproj_even_kernel,
        grid=(r // tm,),
        in_specs=[rows(wa), rows(wb), rows(wb), rows(d),
                  pl.BlockSpec((None, mr, 3 * d), lambda i: (mod_index(i), 0, 0)),
                  full(glu_w), full(glu_b), full(w1), full(w2)],
        out_specs=rows(d),
        out_shape=jax.ShapeDtypeStruct((r, d), F32),
        compiler_params=_cparams(40, "parallel"),
        name="outproj_even",
    )(ya, yg, gb, x, mod, glu_w, glu_b, w1, w2)


def _outproj_odd_kernel(o_ref, x_ref, mod_ref, w_ref, *rest, final):
    d = x_ref.shape[1]
    y = jnp.dot(o_ref[...], w_ref[...], preferred_element_type=F32)
    x = x_ref[...] + _scale_rows(y, mod_ref[...][:, 2 * d:])
    if final:
        g_ref, out_ref = rest
        x = x * lax.rsqrt(jnp.mean(x * x, axis=-1, keepdims=True) + NORM_EPS) * g_ref[...]
    else:
        (out_ref,) = rest
    out_ref[...] = x


def _outproj_odd(o, x, mod, w, tm, blocks_per_sample):
    r, d = x.shape
    mod_index = lambda i: jnp.where(i % blocks_per_sample == 0, mod.shape[0] // 2, 0) + i // blocks_per_sample
    return pl.pallas_call(
        functools.partial(_outproj_odd_kernel, final=False),
        grid=(r // tm,),
        in_specs=[pl.BlockSpec((tm, o.shape[1]), lambda i: (i, 0)),
                  pl.BlockSpec((tm, d), lambda i: (i, 0)),
                  pl.BlockSpec((None, 1, 3 * d), lambda i: (mod_index(i), 0, 0)),
                  pl.BlockSpec(w.shape, lambda i: (0, 0))],
        out_specs=pl.BlockSpec((tm, d), lambda i: (i, 0)),
        out_shape=jax.ShapeDtypeStruct((r, d), F32),
        compiler_params=_cparams(32, "parallel"),
        name="outproj_odd",
    )(o, x, mod, w)


def _outproj_final(o, x, mod, w, final_g, tm, bsz, blocks_per_sample):
    r, d = x.shape
    nlat = blocks_per_sample - 1
    rblk = lambda b, j: (b * blocks_per_sample + 1 + j, 0)
    return pl.pallas_call(
        functools.partial(_outproj_odd_kernel, final=True),
        grid=(bsz, nlat),
        in_specs=[pl.BlockSpec((tm, o.shape[1]), rblk),
                  pl.BlockSpec((tm, d), rblk),
                  pl.BlockSpec((None, 1, 3 * d), lambda b, j: (b, 0, 0)),
                  pl.BlockSpec(w.shape, lambda b, j: (0, 0)),
                  pl.BlockSpec((1, d), lambda b, j: (0, 0))],
        out_specs=pl.BlockSpec((None, tm, d), lambda b, j: (b, j, 0)),
        out_shape=jax.ShapeDtypeStruct((bsz, nlat * tm, d), F32),
        compiler_params=_cparams(32, "parallel", "parallel"),
        name="outproj_final",
    )(o, x, mod, w, final_g.reshape(1, d))


def _rms(x, g):
    return x * lax.rsqrt(jnp.mean(x * x, axis=-1, keepdims=True) + NORM_EPS) * g


def _rope_fold(v, tab):
    w = v * tab
    return w + pltpu.roll(w, LANE // 2, axis=1)


def _mla_up_kernel(cq_ref, ckv_ref, kr_ref, tab_ref, qg_ref, kvg_ref, wq_ref, wkv_ref, q_ref, k_ref, v_ref):
    heads = q_ref.shape[0]
    tab = tab_ref[...]
    q = jnp.dot(_rms(cq_ref[...], qg_ref[...]).astype(BF16), wq_ref[...], preferred_element_type=F32)
    kv = jnp.dot(_rms(ckv_ref[...], kvg_ref[...]).astype(BF16), wkv_ref[...], preferred_element_type=F32)
    kr = _rope_fold(kr_ref[...], tab)[:, :MLA_ROPE].astype(BF16)
    per = MLA_NOPE + 2 * MLA_ROPE
    for h in range(heads):
        qh = q[:, h * per:(h + 1) * per]
        q_ref[h, :, 0:MLA_NOPE] = (qh[:, :MLA_NOPE] * Q_PRESCALE).astype(BF16)
        qr = _rope_fold(qh[:, MLA_NOPE:], tab)[:, :MLA_ROPE] * Q_PRESCALE
        q_ref[h, :, MLA_NOPE:MLA_NOPE + MLA_ROPE] = qr.astype(BF16)
        kvh = kv[:, h * (MLA_NOPE + MLA_V):(h + 1) * (MLA_NOPE + MLA_V)]
        k_ref[h, :, 0:MLA_NOPE] = kvh[:, :MLA_NOPE].astype(BF16)
        k_ref[h, :, MLA_NOPE:MLA_NOPE + MLA_ROPE] = kr
        v_ref[h] = kvh[:, MLA_NOPE:].astype(BF16)


def _mla_up(cq, ckv, krr, tab, q_norm, kv_norm, wq, wkv, tm, blocks_per_sample):
    r = cq.shape[0]
    heads = MLA_HEADS
    dk = MLA_NOPE + MLA_ROPE
    rows = lambda n: pl.BlockSpec((tm, n), lambda i: (i, 0))
    full = lambda a: pl.BlockSpec(a.shape, lambda i: (0, 0))
    hrows = lambda n: pl.BlockSpec((heads, tm, n), lambda i: (0, i, 0))
    return pl.pallas_call(
        _mla_up_kernel,
        grid=(r // tm,),
        in_specs=[rows(cq.shape[1]), rows(ckv.shape[1]), rows(krr.shape[1]),
                  pl.BlockSpec((tm, LANE), lambda i: (i % blocks_per_sample, 0)),
                  full(q_norm), full(kv_norm), full(wq), full(wkv)],
        out_specs=[hrows(dk), hrows(dk), hrows(MLA_V)],
        out_shape=[jax.ShapeDtypeStruct((heads, r, dk), BF16), jax.ShapeDtypeStruct((heads, r, dk), BF16),
                   jax.ShapeDtypeStruct((heads, r, MLA_V), BF16)],
        compiler_params=_cparams(40, "parallel"),
        name="mla_up",
    )(cq, ckv, krr, tab, q_norm, kv_norm, wq, wkv)


def _attn_kernel(q_ref, k_ref, v_ref, g_ref, o_ref, vx_ref, s0_ref, s1_ref, *, with_ctx, n_ctx, tq):
    per = k_ref.shape[0]
    n_lat = (per - n_ctx) // tq
    vx_ref[:, 0:MLA_V] = v_ref[...]
    vx_ref[:, MLA_V:] = jnp.ones((per, MLA_V), BF16)

    def scores(r0, s_ref, nk):
        s_ref[:, 0:nk] = lax.dot_general(q_ref[pl.ds(r0, tq), :], k_ref[0:nk, :], (((1,), (1,)), ((), ())),
                                         preferred_element_type=F32)

    def finish(r0, s_ref, nk):
        s = s_ref[:, 0:nk]
        p = jnp.exp2(s - jnp.max(s, axis=-1, keepdims=True))
        ox = jnp.dot(p.astype(BF16), vx_ref[0:nk, :], preferred_element_type=F32)
        o = ox[:, :MLA_V] / ox[:, MLA_V:]
        g = g_ref[pl.ds(r0, tq), :].astype(F32)
        o_ref[pl.ds(r0, tq), :] = (o * _silu(g)).astype(o_ref.dtype)

    if with_ctx:
        scores(0, s0_ref, n_ctx)
        finish(0, s0_ref, n_ctx)
    else:
        o_ref[0:n_ctx, :] = jnp.zeros((n_ctx, MLA_V), o_ref.dtype)

    row0 = lambda j: pl.multiple_of(n_ctx + j * tq, tq)
    scores(row0(0), s0_ref, per)

    def block_pair(i, carry):
        scores(row0(2 * i + 1), s1_ref, per)
        finish(row0(2 * i), s0_ref, per)
        scores(row0(jnp.minimum(2 * i + 2, n_lat - 1)), s0_ref, per)
        finish(row0(2 * i + 1), s1_ref, per)
        return carry

    lax.fori_loop(0, n_lat // 2, block_pair, 0)


def _attention(q, k, v, gate, bsz, tq, n_ctx, with_ctx):
    heads, r, dk = q.shape
    per = r // bsz
    assert ((per - n_ctx) // tq) % 2 == 0, "latent query blocks are processed in pairs"
    blk = lambda n: pl.BlockSpec((None, per, n), lambda b, h: (h, b, 0))
    return pl.pallas_call(
        functools.partial(_attn_kernel, with_ctx=with_ctx, n_ctx=n_ctx, tq=tq),
        grid=(bsz, heads),
        in_specs=[blk(dk), blk(dk), blk(MLA_V), pl.BlockSpec((per, MLA_V), lambda b, h: (b, h))],
        out_specs=pl.BlockSpec((per, MLA_V), lambda b, h: (b, h)),
        out_shape=jax.ShapeDtypeStruct((r, heads * MLA_V), BF16),
        scratch_shapes=[pltpu.VMEM((per, 2 * MLA_V), BF16), pltpu.VMEM((tq, per), F32), pltpu.VMEM((tq, per), F32)],
        compiler_params=_cparams(40, "parallel", "parallel"),
        name="mla_attention",
    )(q, k, v, gate)


def _s5_matrices(a_re, a_im, bb_re, bb_im, c_re, c_im):
    gps = LANE // S5_GROUP
    nd, g, p, h = bb_re.shape
    slabs = g // gps
    eye = jnp.eye(gps, dtype=F32)
    bd_in = lambda m: jnp.einsum("dsgpj,gk->dsgjkp", m.reshape(nd, slabs, gps, p, h), eye).reshape(nd, slabs, gps * h, gps * p)
    bd_out = lambda m: jnp.einsum("dsgip,gk->dskpgi", m.reshape(nd, slabs, gps, h, p), eye).reshape(nd, slabs, gps * p, gps * h)
    bm = jnp.concatenate([bd_in(bb_re), bd_in(bb_im)], axis=-1).astype(BF16)
    cm = jnp.concatenate([bd_out(c_re), -bd_out(c_im)], axis=-2).astype(BF16)
    ar = a_re.reshape(nd, slabs, 1, gps * p)
    ai = a_im.reshape(nd, slabs, 1, gps * p)
    return bm, cm, ar, ai


def _rope_table(n_ctx, n_lat):
    axis = MLA_ROPE // 2
    rows = n_lat // GRID_W
    row = jnp.repeat(jnp.arange(rows, dtype=F32), GRID_W)
    col = jnp.tile(jnp.arange(GRID_W, dtype=F32), rows)
    inv = ROPE_BASE ** (-jnp.arange(0, axis, 2, dtype=F32) / axis)
    ang = jnp.concatenate([row[:, None] * inv, col[:, None] * inv], axis=-1)
    cos, sin = jnp.cos(ang), jnp.sin(ang)
    lat = jnp.concatenate([cos, cos, -sin, sin], axis=-1)
    one, zero = jnp.ones((n_ctx, axis), F32), jnp.zeros((n_ctx, axis), F32)
    return jnp.concatenate([jnp.concatenate([one, one, zero, zero], axis=-1), lat], axis=0)


def _swap_halves(w):
    half = w.shape[-1] // 2
    return jnp.concatenate([w[..., half:], w[..., :half]], axis=-1)


def kernel(x, c, ctx, c_ctx, norm_g, mod_w, mod_b, ev_w_in, lru_conv_w, lru_conv_b, lru_wr, lru_br, lru_wi, lru_bi, lru_lam, s5_lam_re, s5_lam_im, s5_log_dt, s5_b_re, s5_b_im, s5_c_re, s5_c_im, s5_d, s5_glu_w, s5_glu_b, ev_w_out, mla_w_in, mla_q_norm, mla_w_uq, mla_kv_norm, mla_w_ukv, mla_w_out, final_g):
    bsz, n_lat, d = x.shape
    n_ctx = ctx.shape[1]
    steps = n_ctx + n_lat
    depth = mod_w.shape[0]
    assert bsz == SUBLANE, "time-major layout maps the batch onto the sublanes of one vreg"
    lru_w = lru_conv_w.shape[-1]
    s5_w = s5_d.shape[-2] * s5_d.shape[-1]
    heads = MLA_HEADS
    q_rank, kv_rank = mla_q_norm.shape[-1], mla_kv_norm.shape[-1]

    to_tm = lambda v: v.reshape(bsz, steps, d).transpose(1, 0, 2).reshape(steps * bsz, d)
    to_bm = lambda v: v.reshape(steps, bsz, d).transpose(1, 0, 2).reshape(bsz * steps, d)

    cond = jnp.concatenate([c, jnp.broadcast_to(c_ctx[None], (bsz, d))], axis=0)
    mod = _modulation(cond, mod_w, mod_b)
    mod_tm = mod.reshape(depth, 2, bsz, 3 * d)
    mod_bm = mod.reshape(depth, 2 * bsz, 1, 3 * d)

    a_re, a_im, bb_re, bb_im = _s5_discretise(s5_lam_re, s5_lam_im, s5_log_dt, s5_b_re, s5_b_im)
    rope_tab = _rope_table(n_ctx, n_lat)

    tm_rows = 512
    ctx_blocks_tm = n_ctx * bsz // tm_rows
    tm_index = lambda i: jnp.where(i < ctx_blocks_tm, 1, 0)
    tq = n_ctx
    blocks_per_sample = steps // tq
    bm_index = lambda i: jnp.where(i % blocks_per_sample == 0, bsz, 0) + i // blocks_per_sample

    xs = jnp.concatenate([ctx, x], axis=1).reshape(bsz * steps, d)
    out = None
    for l in range(depth):
        if l % 2 == 0:
            e = l // 2
            xt = to_tm(xs)
            xa, ga, ub, gb = _inproj(xt, mod_tm[l], tm_index, norm_g[l], ev_w_in[e].astype(BF16),
                                     (lru_w, lru_w, s5_w, s5_w), (BF16,) * 4, tm_rows)
            nh = lru_w // LRU_HEAD_DIM
            wg = jnp.concatenate([lru_wr[e], lru_wi[e]], axis=-1).astype(BF16)
            bg = jnp.concatenate([lru_br[e].reshape(2, nh, LRU_HEAD_DIM), lru_bi[e].reshape(2, nh, LRU_HEAD_DIM)],
                                 axis=-1).reshape(2, 1, 2 * lru_w)
            ya = _lru_mix(xa, ga, lru_conv_w[e], lru_conv_b[e].reshape(1, lru_w), wg, bg,
                          lru_lam[e].reshape(2, 1, lru_w), n_ctx * bsz)
            bm, cm, ar, ai = _s5_matrices(a_re[e], a_im[e], bb_re[e], bb_im[e], s5_c_re[e], s5_c_im[e])
            yg = _s5_mix(ub, bm, cm, ar, ai, s5_d[e].reshape(1, s5_w), n_ctx * bsz)
            w_out = ev_w_out[e].astype(BF16)
            xt = _outproj_even(ya, yg, gb, xt, mod_tm[l], tm_index, s5_glu_w[e].astype(BF16),
                               s5_glu_b[e].reshape(1, s5_w), w_out[:lru_w], w_out[lru_w:], tm_rows)
            xs = to_bm(xt)
        else:
            o = l // 2
            with_ctx = l < depth - 1
            w_in = mla_w_in[o]
            off_kr, off_g = q_rank + kv_rank, q_rank + kv_rank + MLA_ROPE
            w_kr = w_in[:, off_kr:off_g]
            w_in = jnp.concatenate([w_in[:, :off_kr], w_in[:, off_g:], w_kr, _swap_halves(w_kr)], axis=-1).astype(BF16)
            gate_w = heads * MLA_V
            cq, ckv, gate, krr = _inproj(xs, mod_bm[l], bm_index, norm_g[l], w_in,
                                         (q_rank, kv_rank, gate_w, 2 * MLA_ROPE), (F32, F32, BF16, F32), tq)
            wq = mla_w_uq[o].reshape(q_rank, heads, MLA_NOPE + MLA_ROPE)
            wq_r = wq[..., MLA_NOPE:]
            wq = jnp.concatenate([wq, _swap_halves(wq_r)], axis=-1).reshape(q_rank, -1).astype(BF16)
            q, k, v = _mla_up(cq, ckv, krr, rope_tab, mla_q_norm[o].reshape(1, q_rank),
                              mla_kv_norm[o].reshape(1, kv_rank), wq, mla_w_ukv[o].astype(BF16), tq, blocks_per_sample)
            att = _attention(q, k, v, gate, bsz, tq, n_ctx, with_ctx)
            w_out = mla_w_out[o].astype(BF16)
            if with_ctx:
                xs = _outproj_odd(att, xs, mod_bm[l], w_out, tq, blocks_per_sample)
            else:
                out = _outproj_final(att, xs, mod_bm[l], w_out, final_g, tq, bsz, blocks_per_sample)
    return out
```

```python
import functools
import math

import jax
import jax.numpy as jnp
from jax import lax
from jax.experimental import pallas as pl
from jax.experimental.pallas import tpu as pltpu

F32 = jnp.float32
BF16 = jnp.bfloat16

NORM_EPS = 1e-6
LRU_C = 8.0
LRU_HEAD_DIM = 128
S5_GROUP = 16
MLA_HEADS = 8
MLA_NOPE = 128
MLA_ROPE = 64
MLA_V = 128
MLA_SCALE = 1.0 / math.sqrt(MLA_NOPE + MLA_ROPE)
LOG2E = math.log2(math.e)
Q_PRESCALE = MLA_SCALE * LOG2E
ROPE_BASE = 10000.0
GRID_W = 64

LANE = 128
SUBLANE = 8
SCAN_STEPS = 32
PROJ_STEPS = 64
MIB = 1024 * 1024


def _cparams(vmem_mib, *sem):
    return pltpu.CompilerParams(dimension_semantics=tuple(sem), vmem_limit_bytes=int(vmem_mib * MIB))


def _sigmoid(x):
    return 0.5 * jnp.tanh(0.5 * x) + 0.5


def _silu(x):
    return x * _sigmoid(x)


def _sqrt_nonneg(x):
    return jnp.where(x > 0.0, x * lax.rsqrt(x), 0.0)


def _rms(x, g):
    return x * lax.rsqrt(jnp.mean(x * x, axis=-1, keepdims=True) + NORM_EPS) * g


def _mod_kernel(c_ref, w_ref, b_ref, o_ref):
    a = _silu(c_ref[...])
    o_ref[...] = jnp.dot(a, w_ref[...], preferred_element_type=F32, precision=lax.Precision.HIGHEST) + b_ref[...]


def _modulation(cond, mod_w, mod_b):
    depth, d, n = mod_w.shape
    rows = cond.shape[0]
    tn = min(n, 1024)
    return pl.pallas_call(
        _mod_kernel,
        grid=(depth, n // tn),
        in_specs=[
            pl.BlockSpec((rows, d), lambda l, j: (0, 0)),
            pl.BlockSpec((None, d, tn), lambda l, j: (l, 0, j)),
            pl.BlockSpec((None, 1, tn), lambda l, j: (l, 0, j)),
        ],
        out_specs=pl.BlockSpec((None, rows, tn), lambda l, j: (l, 0, j)),
        out_shape=jax.ShapeDtypeStruct((depth, rows, n), F32),
        compiler_params=_cparams(32, "parallel", "parallel"),
        name="modulation",
    )(cond, mod_w, mod_b.reshape(depth, 1, n))


def _inproj_kernel(x_ref, mod_ref, g_ref, w_ref, *out_refs, splits):
    x = x_ref[...]
    d = x.shape[-1]
    m = mod_ref[...]
    n = _rms(x, g_ref[...]) * (1.0 + m[..., d:2 * d]) + m[..., :d]
    if n.ndim == 3:
        n = jnp.swapaxes(n, 0, 1).reshape(-1, d)
    n = n.astype(BF16)
    off = 0
    for o_ref, width in zip(out_refs, splits):
        o_ref[...] = jnp.dot(n, w_ref[:, off:off + width], preferred_element_type=F32).astype(o_ref.dtype)
        off += width


def _inproj(x, x_spec, mod5, mod_spec, g3, w3, layer, wsel, splits, dtypes, rows, tm, grid):
    d, n = w3.shape[1:]
    return pl.pallas_call(
        functools.partial(_inproj_kernel, splits=splits),
        grid=(grid,),
        in_specs=[x_spec, mod_spec,
                  pl.BlockSpec((None, 1, d), lambda i: (layer, 0, 0)),
                  pl.BlockSpec((None, d, n), lambda i: (wsel, 0, 0))],
        out_specs=[pl.BlockSpec((tm, s), lambda i: (i, 0)) for s in splits],
        out_shape=[jax.ShapeDtypeStruct((rows, s), dt) for s, dt in zip(splits, dtypes)],
        compiler_params=_cparams(48, "parallel"),
        name="inproj",
    )(x, mod5, g3, w3)


def _softplus(x):
    return jnp.log1p(jnp.exp(-jnp.abs(x))) + jnp.maximum(x, 0.0)


def _chunk_order(j, n_ctx, n_all):
    return jnp.where(j < n_ctx, n_ctx - 1 - j, n_all - 1 - (j - n_ctx))


def _lru_kernel(xa_ref, ga_ref, cw_ref, cb_ref, wg_ref, bg_ref, lam_ref, o_ref,
                hf_ref, hb_ref, *ab_refs, n_ctx, n_all, ch):
    rows_total = xa_ref.shape[0]
    hd = xa_ref.shape[1]
    tb = ch // SUBLANE
    halo = 2 * SUBLANE
    row = lax.broadcasted_iota(jnp.int32, (ch, hd), 0)
    cw = cw_ref[...]
    cb = cb_ref[...]
    bufs = ((ab_refs[0:2], ab_refs[2:4]), (ab_refs[4:6], ab_refs[6:8]))
    dst = (hf_ref, hb_ref)
    order = (lambda j: j, lambda j: _chunk_order(j, n_ctx, n_all))
    k_log2a = [(-0.5 * LRU_C * LOG2E) * _softplus(-lam_ref[d]) for d in (0, 1)]

    def conv(c):
        r0 = pl.multiple_of(c * ch, ch)
        first = jnp.logical_or(c == 0, c == n_ctx)
        last = jnp.logical_or(c == n_ctx - 1, c == n_all - 1)
        rp = pl.multiple_of(jnp.maximum(r0 - halo, 0), halo)
        rn = pl.multiple_of(jnp.minimum(r0 + ch, rows_total - halo), halo)
        ext = jnp.concatenate([xa_ref[pl.ds(rp, halo), :], xa_ref[pl.ds(r0, ch), :],
                               xa_ref[pl.ds(rn, halo), :]], axis=0).astype(F32)
        xm2 = jnp.where(jnp.logical_and(first, row < 2 * SUBLANE), 0.0, ext[0:ch])
        xm1 = jnp.where(jnp.logical_and(first, row < SUBLANE), 0.0, ext[SUBLANE:SUBLANE + ch])
        x0 = ext[2 * SUBLANE:2 * SUBLANE + ch]
        xp1 = jnp.where(jnp.logical_and(last, row >= ch - SUBLANE), 0.0, ext[3 * SUBLANE:3 * SUBLANE + ch])
        return cb + cw[0:1] * xm2 + cw[1:2] * xm1 + cw[2:3] * x0 + cw[3:4] * xp1

    def coeffs(c, d, a_ref, b_ref):
        u = conv(c)
        t = jnp.tanh(jnp.dot(u.astype(BF16), wg_ref[d], preferred_element_type=F32) + bg_ref[d])
        a = jnp.exp2(t[:, :hd] * k_log2a[d] + k_log2a[d])
        a_ref[...] = a
        hu = 0.5 * u
        b_ref[...] = _sqrt_nonneg(1.0 - a * a) * (t[:, hd:] * hu + hu)

    def scan(c, d, a_ref, b_ref, h):
        r0 = pl.multiple_of(c * ch, ch)
        for t in range(tb):
            rr = (t if d == 0 else tb - 1 - t) * SUBLANE
            h = a_ref[rr:rr + SUBLANE, :] * h + b_ref[rr:rr + SUBLANE, :]
            dst[d][pl.ds(r0 + rr, SUBLANE), :] = h
        return h

    for d in (0, 1):
        coeffs(order[d](0), d, *bufs[d][0])

    def chunk_pair(i, hs):
        hs = list(hs)
        for slot in (0, 1):
            j = 2 * i + slot
            nxt = jnp.minimum(j + 1, n_all - 1)
            for d in (0, 1):
                coeffs(order[d](nxt), d, *bufs[d][1 - slot])
            for d in (0, 1):
                hs[d] = scan(order[d](j), d, *bufs[d][slot], hs[d])
        return tuple(hs)

    h0 = jnp.zeros((SUBLANE, hd), F32)
    lax.fori_loop(0, n_all // 2, chunk_pair, (h0, h0))

    def finish(c, carry):
        r0 = pl.multiple_of(c * ch, ch)
        g = ga_ref[pl.ds(r0, ch), :].astype(F32)
        o_ref[pl.ds(r0, ch), :] = ((hf_ref[pl.ds(r0, ch), :] + hb_ref[pl.ds(r0, ch), :]) * _silu(g)).astype(o_ref.dtype)
        return carry

    lax.fori_loop(0, n_all, finish, 0)


def _lru_mix(xa, ga, conv_w, conv_b, wg, bg, lam, e, n_ctx_rows):
    r, w = xa.shape
    hd = LRU_HEAD_DIM
    ch = SCAN_STEPS * SUBLANE
    assert (r // ch) % 2 == 0, "chunks are processed in pairs"
    kern = functools.partial(_lru_kernel, n_ctx=n_ctx_rows // ch, n_all=r // ch, ch=ch)
    return pl.pallas_call(
        kern,
        grid=(w // hd,),
        in_specs=[
            pl.BlockSpec((r, hd), lambda h: (0, h)),
            pl.BlockSpec((r, hd), lambda h: (0, h)),
            pl.BlockSpec((None, conv_w.shape[1], hd), lambda h: (e, 0, h)),
            pl.BlockSpec((None, 1, hd), lambda h: (e, 0, h)),
            pl.BlockSpec((None, 2, None, hd, 2 * hd), lambda h: (e, 0, h, 0, 0)),
            pl.BlockSpec((None, 2, 1, 2 * hd), lambda h: (e, 0, 0, h)),
            pl.BlockSpec((None, 2, 1, hd), lambda h: (e, 0, 0, h)),
        ],
        out_specs=pl.BlockSpec((r, hd), lambda h: (0, h)),
        out_shape=jax.ShapeDtypeStruct((r, w), BF16),
        scratch_shapes=[pltpu.VMEM((r, hd), F32)] * 2 + [pltpu.VMEM((ch, hd), F32)] * 8,
        compiler_params=_cparams(56, "parallel"),
        name="lru_mix",
    )(xa, ga, conv_w, conv_b, wg, bg, lam)


def _s5_disc_kernel(lre_ref, lim_ref, ldt_ref, bre_ref, bim_ref, are_ref, aim_ref, ore_ref, oim_ref):
    lam_re = lre_ref[...]
    lam_im = lim_ref[...]
    dt = jnp.exp(ldt_ref[...])
    mag = jnp.exp(lam_re * dt)
    ab_re = mag * jnp.cos(lam_im * dt)
    ab_im = mag * jnp.sin(lam_im * dt)
    den = lam_re * lam_re + lam_im * lam_im
    nr = ab_re - 1.0
    f_re = (nr * lam_re + ab_im * lam_im) / den
    f_im = (ab_im * lam_re - nr * lam_im) / den
    b_re = bre_ref[...]
    b_im = bim_ref[...]
    are_ref[...] = ab_re
    aim_ref[...] = ab_im
    ore_ref[...] = f_re * b_re - f_im * b_im
    oim_ref[...] = f_re * b_im + f_im * b_re


def _s5_discretise(lam_re, lam_im, log_dt, b_re, b_im):
    lead = b_re.shape[:-2]
    p, h = b_re.shape[-2:]
    nrow = math.prod(lead)
    flat = lambda v: jnp.broadcast_to(v[..., None], b_re.shape).reshape(nrow, p * h)
    shp = jax.ShapeDtypeStruct((nrow, p * h), F32)
    a_re, a_im, o_re, o_im = pl.pallas_call(
        _s5_disc_kernel, out_shape=[shp] * 4, name="s5_discretise",
    )(flat(lam_re), flat(lam_im), flat(log_dt), b_re.reshape(nrow, p * h), b_im.reshape(nrow, p * h))
    unflat = lambda v: v.reshape(*lead, p, h)
    return unflat(a_re)[..., 0], unflat(a_im)[..., 0], unflat(o_re), unflat(o_im)


def _s5_kernel(u_ref, bm_ref, cm_ref, ar_ref, ai_ref, dsk_ref, o_ref, yf_ref, yb_ref, *v_refs, n_ctx, n_all, ch):
    ns = ar_ref.shape[-1]
    tb = ch // SUBLANE
    bufs = (v_refs[0:2], v_refs[2:4])
    h_refs = (v_refs[4:6], v_refs[6:8])
    y_refs = (yf_ref, yb_ref)
    order = (lambda j: j, lambda j: _chunk_order(j, n_ctx, n_all))
    row0 = lambda j, d: pl.multiple_of(order[d](j) * ch, ch)

    def expand(j, slot):
        for d in (0, 1):
            bufs[d][slot][...] = jnp.dot(u_ref[pl.ds(row0(j, d), ch), :], bm_ref[d], preferred_element_type=F32)

    def scan(slot, hs):
        hs = [list(hs[0:2]), list(hs[2:4])]
        ar = [jnp.broadcast_to(ar_ref[d], (SUBLANE, ns)) for d in (0, 1)]
        ai = [jnp.broadcast_to(ai_ref[d], (SUBLANE, ns)) for d in (0, 1)]
        for k in range(tb // 2):
            for d in (0, 1):
                v_ref = bufs[d][slot]
                pair = (2 * k, 2 * k + 1) if d == 0 else (tb - 1 - 2 * k, tb - 2 - 2 * k)
                new = {}
                for t in pair:
                    rr = t * SUBLANE
                    hr, hi = hs[d]
                    nr = ar[d] * hr - ai[d] * hi + v_ref[rr:rr + SUBLANE, 0:ns]
                    ni = ar[d] * hi + ai[d] * hr + v_ref[rr:rr + SUBLANE, ns:2 * ns]
                    hs[d] = new[t] = [nr, ni]
                lo = min(pair)
                rows = slice(lo * SUBLANE, (lo + 2) * SUBLANE)
                for part in (0, 1):
                    both = jnp.concatenate([new[lo][part], new[lo + 1][part]], axis=0)
                    h_refs[d][slot][rows, part * ns:(part + 1) * ns] = both.astype(BF16)
        return (*hs[0], *hs[1])

    def project(j, slot):
        for d in (0, 1):
            y_refs[d][pl.ds(row0(j, d), ch), :] = jnp.dot(h_refs[d][slot][...], cm_ref[d], preferred_element_type=F32)

    last = n_all - 1
    z = jnp.zeros((SUBLANE, ns), F32)
    expand(0, 0)
    hs = scan(0, (z, z, z, z))
    expand(1, 1)

    def chunk_pair(i, hs):
        j = 2 * i
        project(j, 0)
        hs = scan(1, hs)
        expand(jnp.minimum(j + 2, last), 0)
        project(j + 1, 1)
        hs = scan(0, hs)
        expand(jnp.minimum(j + 3, last), 1)
        return hs

    lax.fori_loop(0, n_all // 2, chunk_pair, hs)

    def finish(c, carry):
        r0 = pl.multiple_of(c * ch, ch)
        u = u_ref[pl.ds(r0, ch), :].astype(F32)
        y = yf_ref[pl.ds(r0, ch), :] + yb_ref[pl.ds(r0, ch), :] + dsk_ref[...] * u
        o_ref[pl.ds(r0, ch), :] = jax.nn.gelu(y).astype(o_ref.dtype)
        return carry

    lax.fori_loop(0, n_all, finish, 0)


def _s5_mix(ub, bm, cm, ar, ai, dsk, e, n_ctx_rows):
    r, w = ub.shape
    ns = ar.shape[-1]
    ch = SCAN_STEPS * SUBLANE
    assert (r // ch) % 2 == 0, "chunks are processed in pairs"
    kern = functools.partial(_s5_kernel, n_ctx=n_ctx_rows // ch, n_all=r // ch, ch=ch)
    return pl.pallas_call(
        kern,
        grid=(w // LANE,),
        in_specs=[
            pl.BlockSpec((r, LANE), lambda s: (0, s)),
            pl.BlockSpec((None, 2, None, LANE, 2 * ns), lambda s: (e, 0, s, 0, 0)),
            pl.BlockSpec((None, 2, None, 2 * ns, LANE), lambda s: (e, 0, s, 0, 0)),
            pl.BlockSpec((None, 2, None, 1, ns), lambda s: (e, 0, s, 0, 0)),
            pl.BlockSpec((None, 2, None, 1, ns), lambda s: (e, 0, s, 0, 0)),
            pl.BlockSpec((None, 1, LANE), lambda s: (e, 0, s)),
        ],
        out_specs=pl.BlockSpec((r, LANE), lambda s: (0, s)),
        out_shape=jax.ShapeDtypeStruct((r, w), BF16),
        scratch_shapes=([pltpu.VMEM((r, LANE), F32)] * 2 + [pltpu.VMEM((ch, 2 * ns), F32)] * 4
                        + [pltpu.VMEM((ch, 2 * ns), BF16)] * 4),
        compiler_params=_cparams(54, "parallel"),
        name="s5_mix",
    )(ub, bm, cm, ar, ai, dsk)


def _outproj_even_kernel(ya_ref, yg_ref, gb_ref, x_ref, mod_ref, gw_ref, gbias_ref, w1_ref, w2_ref, o_ref):
    bsz, ts, d = x_ref.shape
    yg = yg_ref[...]
    z = jnp.dot(yg, gw_ref[...], preferred_element_type=F32) + gbias_ref[...]
    yb = yg.astype(F32) * _sigmoid(z) * _silu(gb_ref[...].astype(F32))
    o = jnp.dot(ya_ref[...], w1_ref[...], preferred_element_type=F32)
    o = o + jnp.dot(yb.astype(BF16), w2_ref[...], preferred_element_type=F32)
    o = jnp.swapaxes(o.reshape(ts, bsz, d), 0, 1)
    o_ref[...] = x_ref[...] + mod_ref[...][..., 2 * d:] * o


def _outproj_even(ya, yg, gb, xs, mod5, layer, seg, glu_w, glu_b, w_out, e, ts):
    bsz, steps, d = xs.shape
    wa, wb = ya.shape[1], yg.shape[1]
    tm = ts * bsz
    rows = lambda n: pl.BlockSpec((tm, n), lambda i: (i, 0))
    xblk = pl.BlockSpec((bsz, ts, d), lambda i: (0, i, 0))
    return pl.pallas_call(
        _outproj_even_kernel,
        grid=(steps // ts,),
        in_specs=[rows(wa), rows(wb), rows(wb), xblk,
                  pl.BlockSpec((None, None, bsz, 1, 3 * d), lambda i: (layer, seg(i), 0, 0, 0)),
                  pl.BlockSpec((None, wb, wb), lambda i: (e, 0, 0)),
                  pl.BlockSpec((None, 1, wb), lambda i: (e, 0, 0)),
                  pl.BlockSpec((None, wa, d), lambda i: (e, 0, 0)),
                  pl.BlockSpec((None, wb, d), lambda i: (e, wa // wb, 0))],
        out_specs=xblk,
        out_shape=jax.ShapeDtypeStruct(xs.shape, F32),
        compiler_params=_cparams(40, "parallel"),
        name="outproj_even",
    )(ya, yg, gb, xs, mod5, glu_w, glu_b, w_out, w_out)


def _outproj_odd_kernel(o_ref, x_ref, mod_ref, w_ref, *rest, final):
    d = x_ref.shape[1]
    y = jnp.dot(o_ref[...], w_ref[...], preferred_element_type=F32)
    x = x_ref[...] + mod_ref[...][:, 2 * d:] * y
    if final:
        g_ref, out_ref = rest
        x = _rms(x, g_ref[...])
    else:
        (out_ref,) = rest
    out_ref[...] = x


def _outproj_odd(att, xs2, mod5, layer, w, o, tm, blocks_per_sample, bsz, final_g=None):
    r, d = xs2.shape
    wspec = pl.BlockSpec((None, w.shape[1], d), lambda *_: (o, 0, 0))
    if final_g is None:
        blk = lambda n: pl.BlockSpec((tm, n), lambda i: (i, 0))
        mod_spec = pl.BlockSpec((None, None, None, 1, 3 * d),
                                lambda i: (layer, (i % blocks_per_sample == 0).astype(jnp.int32), i // blocks_per_sample, 0, 0))
        return pl.pallas_call(
            functools.partial(_outproj_odd_kernel, final=False),
            grid=(r // tm,),
            in_specs=[blk(att.shape[1]), blk(d), mod_spec, wspec],
            out_specs=blk(d),
            out_shape=jax.ShapeDtypeStruct((r, d), F32),
            compiler_params=_cparams(32, "parallel"),
            name="outproj_odd",
        )(att, xs2, mod5, w)
    nlat = blocks_per_sample - 1
    blk = lambda n: pl.BlockSpec((tm, n), lambda b, j: (b * blocks_per_sample + 1 + j, 0))
    return pl.pallas_call(
        functools.partial(_outproj_odd_kernel, final=True),
        grid=(bsz, nlat),
        in_specs=[blk(att.shape[1]), blk(d),
                  pl.BlockSpec((None, None, None, 1, 3 * d), lambda b, j: (layer, 0, b, 0, 0)),
                  wspec, pl.BlockSpec((1, d), lambda b, j: (0, 0))],
        out_specs=pl.BlockSpec((None, tm, d), lambda b, j: (b, j, 0)),
        out_shape=jax.ShapeDtypeStruct((bsz, nlat * tm, d), F32),
        compiler_params=_cparams(32, "parallel", "parallel"),
        name="outproj_final",
    )(att, xs2, mod5, w, final_g.reshape(1, d))


def _rope_fold(v, tab):
    w = v * tab
    return w + pltpu.roll(w, LANE // 2, axis=1)


def _mla_up_kernel(cq_ref, ckv_ref, kr_ref, tab_ref, qg_ref, kvg_ref, wq_ref, wkv_ref, q_ref, k_ref, v_ref):
    heads = q_ref.shape[0]
    tab = tab_ref[...]
    q = jnp.dot(_rms(cq_ref[...], qg_ref[...]).astype(BF16), wq_ref[...], preferred_element_type=F32)
    kv = jnp.dot(_rms(ckv_ref[...], kvg_ref[...]).astype(BF16), wkv_ref[...], preferred_element_type=F32)
    kr = _rope_fold(kr_ref[...], tab)[:, :MLA_ROPE].astype(BF16)
    per = MLA_NOPE + 2 * MLA_ROPE
    for h in range(heads):
        qh = q[:, h * per:(h + 1) * per]
        q_ref[h, :, 0:MLA_NOPE] = (qh[:, :MLA_NOPE] * Q_PRESCALE).astype(BF16)
        qr = _rope_fold(qh[:, MLA_NOPE:], tab)[:, :MLA_ROPE] * Q_PRESCALE
        q_ref[h, :, MLA_NOPE:MLA_NOPE + MLA_ROPE] = qr.astype(BF16)
        kvh = kv[:, h * (MLA_NOPE + MLA_V):(h + 1) * (MLA_NOPE + MLA_V)]
        k_ref[h, :, 0:MLA_NOPE] = kvh[:, :MLA_NOPE].astype(BF16)
        k_ref[h, :, MLA_NOPE:MLA_NOPE + MLA_ROPE] = kr
        v_ref[h] = kvh[:, MLA_NOPE:].astype(BF16)


def _mla_up(cq, ckv, krr, tab, q_norm, kv_norm, wq, wkv, o, tm, blocks_per_sample):
    r = cq.shape[0]
    heads = MLA_HEADS
    dk = MLA_NOPE + MLA_ROPE
    rows = lambda n: pl.BlockSpec((tm, n), lambda i: (i, 0))
    layer = lambda a: pl.BlockSpec((None,) + a.shape[1:], lambda i: (o, 0, 0))
    hrows = lambda n: pl.BlockSpec((heads, tm, n), lambda i: (0, i, 0))
    return pl.pallas_call(
        _mla_up_kernel,
        grid=(r // tm,),
        in_specs=[rows(cq.shape[1]), rows(ckv.shape[1]), rows(krr.shape[1]),
                  pl.BlockSpec((tm, LANE), lambda i: (i % blocks_per_sample, 0)),
                  layer(q_norm), layer(kv_norm), layer(wq), layer(wkv)],
        out_specs=[hrows(dk), hrows(dk), hrows(MLA_V)],
        out_shape=[jax.ShapeDtypeStruct((heads, r, dk), BF16), jax.ShapeDtypeStruct((heads, r, dk), BF16),
                   jax.ShapeDtypeStruct((heads, r, MLA_V), BF16)],
        compiler_params=_cparams(40, "parallel"),
        name="mla_up",
    )(cq, ckv, krr, tab, q_norm, kv_norm, wq, wkv)


def _attn_kernel(q_ref, k_ref, v_ref, g_ref, o_ref, vx_ref, s0_ref, s1_ref, *, with_ctx, n_ctx, tq):
    per = k_ref.shape[0]
    n_lat = (per - n_ctx) // tq
    vx_ref[:, 0:MLA_V] = v_ref[...]
    vx_ref[:, MLA_V:] = jnp.ones((per, MLA_V), BF16)

    def scores(r0, s_ref, nk):
        s_ref[:, 0:nk] = lax.dot_general(q_ref[pl.ds(r0, tq), :], k_ref[0:nk, :], (((1,), (1,)), ((), ())),
                                         preferred_element_type=F32)

    def finish(r0, s_ref, nk):
        s = s_ref[:, 0:nk]
        p = jnp.exp2(s - jnp.max(s, axis=-1, keepdims=True))
        ox = jnp.dot(p.astype(BF16), vx_ref[0:nk, :], preferred_element_type=F32)
        o = ox[:, :MLA_V] / ox[:, MLA_V:]
        g = g_ref[pl.ds(r0, tq), :].astype(F32)
        o_ref[pl.ds(r0, tq), :] = (o * _silu(g)).astype(o_ref.dtype)

    if with_ctx:
        scores(0, s0_ref, n_ctx)
        finish(0, s0_ref, n_ctx)
    else:
        o_ref[0:n_ctx, :] = jnp.zeros((n_ctx, MLA_V), o_ref.dtype)

    row0 = lambda j: pl.multiple_of(n_ctx + j * tq, tq)
    scores(row0(0), s0_ref, per)

    def block_pair(i, carry):
        scores(row0(2 * i + 1), s1_ref, per)
        finish(row0(2 * i), s0_ref, per)
        scores(row0(jnp.minimum(2 * i + 2, n_lat - 1)), s0_ref, per)
        finish(row0(2 * i + 1), s1_ref, per)
        return carry

    lax.fori_loop(0, n_lat // 2, block_pair, 0)


def _attention(q, k, v, gate, bsz, tq, n_ctx, with_ctx):
    heads, r, dk = q.shape
    per = r // bsz
    assert ((per - n_ctx) // tq) % 2 == 0, "latent query blocks are processed in pairs"
    blk = lambda n: pl.BlockSpec((None, per, n), lambda b, h: (h, b, 0))
    return pl.pallas_call(
        functools.partial(_attn_kernel, with_ctx=with_ctx, n_ctx=n_ctx, tq=tq),
        grid=(bsz, heads),
        in_specs=[blk(dk), blk(dk), blk(MLA_V), pl.BlockSpec((per, MLA_V), lambda b, h: (b, h))],
        out_specs=pl.BlockSpec((per, MLA_V), lambda b, h: (b, h)),
        out_shape=jax.ShapeDtypeStruct((r, heads * MLA_V), BF16),
        scratch_shapes=[pltpu.VMEM((per, 2 * MLA_V), BF16), pltpu.VMEM((tq, per), F32), pltpu.VMEM((tq, per), F32)],
        compiler_params=_cparams(40, "parallel", "parallel"),
        name="mla_attention",
    )(q, k, v, gate)


def _s5_matrices(a_re, a_im, bb_re, bb_im, c_re, c_im):
    gps = LANE // S5_GROUP
    ne, nd, g, p, h = bb_re.shape
    slabs = g // gps
    eye = jnp.eye(gps, dtype=F32)
    bd_in = lambda m: jnp.einsum("edsgpj,gk->edsgjkp", m.reshape(ne, nd, slabs, gps, p, h), eye).reshape(
        ne, nd, slabs, gps * h, gps * p)
    bd_out = lambda m: jnp.einsum("edsgip,gk->edskpgi", m.reshape(ne, nd, slabs, gps, h, p), eye).reshape(
        ne, nd, slabs, gps * p, gps * h)
    bm = jnp.concatenate([bd_in(bb_re), bd_in(bb_im)], axis=-1).astype(BF16)
    cm = jnp.concatenate([bd_out(c_re), -bd_out(c_im)], axis=-2).astype(BF16)
    ar = a_re.reshape(ne, nd, slabs, 1, gps * p)
    ai = a_im.reshape(ne, nd, slabs, 1, gps * p)
    return bm, cm, ar, ai


def _rope_table(n_ctx, n_lat):
    axis = MLA_ROPE // 2
    rows = n_lat // GRID_W
    row = jnp.repeat(jnp.arange(rows, dtype=F32), GRID_W)
    col = jnp.tile(jnp.arange(GRID_W, dtype=F32), rows)
    inv = ROPE_BASE ** (-jnp.arange(0, axis, 2, dtype=F32) / axis)
    ang = jnp.concatenate([row[:, None] * inv, col[:, None] * inv], axis=-1)
    cos, sin = jnp.cos(ang), jnp.sin(ang)
    lat = jnp.concatenate([cos, cos, -sin, sin], axis=-1)
    one, zero = jnp.ones((n_ctx, axis), F32), jnp.zeros((n_ctx, axis), F32)
    return jnp.concatenate([jnp.concatenate([one, one, zero, zero], axis=-1), lat], axis=0)


def _swap_halves(w):
    half = w.shape[-1] // 2
    return jnp.concatenate([w[..., half:], w[..., :half]], axis=-1)


def kernel(x, c, ctx, c_ctx, norm_g, mod_w, mod_b, ev_w_in, lru_conv_w, lru_conv_b, lru_wr, lru_br, lru_wi, lru_bi, lru_lam, s5_lam_re, s5_lam_im, s5_log_dt, s5_b_re, s5_b_im, s5_c_re, s5_c_im, s5_d, s5_glu_w, s5_glu_b, ev_w_out, mla_w_in, mla_q_norm, mla_w_uq, mla_kv_norm, mla_w_ukv, mla_w_out, final_g):
    bsz, n_lat, d = x.shape
    n_ctx = ctx.shape[1]
    steps = n_ctx + n_lat
    depth = mod_w.shape[0]
    n_even, n_odd = ev_w_in.shape[0], mla_w_in.shape[0]
    assert bsz == SUBLANE, "time-major layout maps the batch onto the sublanes of one vreg"
    assert depth % 2 == 0, "the last layer is an attention layer"
    lru_w = lru_conv_w.shape[-1]
    s5_w = s5_d.shape[-2] * s5_d.shape[-1]
    nh = lru_w // LRU_HEAD_DIM
    heads = MLA_HEADS
    q_rank, kv_rank = mla_q_norm.shape[-1], mla_kv_norm.shape[-1]

    cond = jnp.concatenate([c, jnp.broadcast_to(c_ctx[None], (bsz, d))], axis=0)
    mod5 = _modulation(cond, mod_w, mod_b).reshape(depth, 2, bsz, 1, 3 * d)
    norm_g3 = norm_g.reshape(depth, 1, d)

    ev_w_in_b = ev_w_in.astype(BF16)
    ev_w_out_b = ev_w_out.astype(BF16)
    glu_w_b = s5_glu_w.astype(BF16)
    glu_b3 = s5_glu_b.reshape(n_even, 1, s5_w)
    conv_b3 = lru_conv_b.reshape(n_even, 1, lru_w)
    wg = (0.5 * jnp.concatenate([lru_wr, lru_wi], axis=-1)).astype(BF16)
    bg = 0.5 * jnp.concatenate([lru_br.reshape(n_even, 2, nh, LRU_HEAD_DIM), lru_bi.reshape(n_even, 2, nh, LRU_HEAD_DIM)],
                               axis=-1).reshape(n_even, 2, 1, 2 * lru_w)
    lam4 = lru_lam.reshape(n_even, 2, 1, lru_w)
    a_re, a_im, bb_re, bb_im = _s5_discretise(s5_lam_re, s5_lam_im, s5_log_dt, s5_b_re, s5_b_im)
    s5_bm, s5_cm, s5_ar, s5_ai = _s5_matrices(a_re, a_im, bb_re, bb_im, s5_c_re, s5_c_im)
    dsk3 = s5_d.reshape(n_even, 1, s5_w)

    off_kr, off_g = q_rank + kv_rank, q_rank + kv_rank + MLA_ROPE
    w_kr = mla_w_in[..., off_kr:off_g]
    mla_w_in_b = jnp.concatenate([mla_w_in[..., :off_kr], mla_w_in[..., off_g:], w_kr, _swap_halves(w_kr)],
                                 axis=-1).astype(BF16)
    wq = mla_w_uq.reshape(n_odd, q_rank, heads, MLA_NOPE + MLA_ROPE)
    wq_b = jnp.concatenate([wq, _swap_halves(wq[..., MLA_NOPE:])], axis=-1).reshape(n_odd, q_rank, -1).astype(BF16)
    wkv_b = mla_w_ukv.astype(BF16)
    mla_w_out_b = mla_w_out.astype(BF16)
    q_norm3 = mla_q_norm.reshape(n_odd, 1, q_rank)
    kv_norm3 = mla_kv_norm.reshape(n_odd, 1, kv_rank)
    rope_tab = _rope_table(n_ctx, n_lat)

    ts = PROJ_STEPS
    ctx_blocks = n_ctx // ts
    seg_tm = lambda i: (i < ctx_blocks).astype(jnp.int32)
    tq = n_ctx
    blocks_per_sample = steps // tq
    rows = bsz * steps

    xs = jnp.concatenate([ctx, x], axis=1)
    for l in range(depth):
        if l % 2 == 0:
            e = l // 2
            x_spec = pl.BlockSpec((bsz, ts, d), lambda i: (0, i, 0))
            mod_spec = pl.BlockSpec((None, None, bsz, 1, 3 * d), lambda i: (l, seg_tm(i), 0, 0, 0))
            xa, ga, ub, gb = _inproj(xs, x_spec, mod5, mod_spec, norm_g3, ev_w_in_b, l, e,
                                     (lru_w, lru_w, s5_w, s5_w), (BF16,) * 4, rows, ts * bsz, steps // ts)
            ya = _lru_mix(xa, ga, lru_conv_w, conv_b3, wg, bg, lam4, e, n_ctx * bsz)
            yg = _s5_mix(ub, s5_bm, s5_cm, s5_ar, s5_ai, dsk3, e, n_ctx * bsz)
            xs = _outproj_even(ya, yg, gb, xs, mod5, l, seg_tm, glu_w_b, glu_b3, ev_w_out_b, e, ts)
        else:
            o = l // 2
            with_ctx = l < depth - 1
            xs2 = xs.reshape(rows, d)
            x_spec = pl.BlockSpec((tq, d), lambda i: (i, 0))
            mod_spec = pl.BlockSpec(
                (None, None, None, 1, 3 * d),
                lambda i: (l, (i % blocks_per_sample == 0).astype(jnp.int32), i // blocks_per_sample, 0, 0))
            cq, ckv, gate, krr = _inproj(xs2, x_spec, mod5, mod_spec, norm_g3, mla_w_in_b, l, o,
                                         (q_rank, kv_rank, heads * MLA_V, 2 * MLA_ROPE), (F32, F32, BF16, F32),
                                         rows, tq, rows // tq)
            q, k, v = _mla_up(cq, ckv, krr, rope_tab, q_norm3, kv_norm3, wq_b, wkv_b, o, tq, blocks_per_sample)
            att = _attention(q, k, v, gate, bsz, tq, n_ctx, with_ctx)
            if with_ctx:
                xs = _outproj_odd(att, xs2, mod5, l, mla_w_out_b, o, tq, blocks_per_sample, bsz).reshape(bsz, steps, d)
            else:
                xs = _outproj_odd(att, xs2, mod5, l, mla_w_out_b, o, tq, blocks_per_sample, bsz, final_g=final_g)
    return xs
```

```python
import functools
import math

import jax
import jax.numpy as jnp
from jax import lax
from jax.experimental import pallas as pl
from jax.experimental.pallas import tpu as pltpu

F32 = jnp.float32
BF16 = jnp.bfloat16

NORM_EPS = 1e-6
LRU_C = 8.0
LRU_HEAD_DIM = 128
S5_GROUP = 16
MLA_HEADS = 8
MLA_NOPE = 128
MLA_ROPE = 64
MLA_V = 128
MLA_SCALE = 1.0 / math.sqrt(MLA_NOPE + MLA_ROPE)
LOG2E = math.log2(math.e)
Q_PRESCALE = MLA_SCALE * LOG2E
ROPE_BASE = 10000.0
GRID_W = 64

LANE = 128
SUBLANE = 8
SCAN_STEPS = 32
PROJ_STEPS = 64
ATTN_Q_ROWS = 512
MIB = 1024 * 1024


def _cparams(vmem_mib, *sem):
    return pltpu.CompilerParams(dimension_semantics=tuple(sem), vmem_limit_bytes=int(vmem_mib * MIB))


def _sigmoid(x):
    return 0.5 * jnp.tanh(0.5 * x) + 0.5


def _silu(x):
    return x * _sigmoid(x)


def _sqrt_nonneg(x):
    return jnp.where(x > 0.0, x * lax.rsqrt(x), 0.0)


def _rms(x, g):
    return x * lax.rsqrt(jnp.mean(x * x, axis=-1, keepdims=True) + NORM_EPS) * g


def _mod_kernel(c_ref, w_ref, b_ref, o_ref):
    a = _silu(c_ref[...])
    o_ref[...] = jnp.dot(a, w_ref[...], preferred_element_type=F32, precision=lax.Precision.HIGHEST) + b_ref[...]


def _modulation(cond, mod_w, mod_b):
    depth, d, n = mod_w.shape
    rows = cond.shape[0]
    tn = min(n, 1024)
    return pl.pallas_call(
        _mod_kernel,
        grid=(depth, n // tn),
        in_specs=[
            pl.BlockSpec((rows, d), lambda l, j: (0, 0)),
            pl.BlockSpec((None, d, tn), lambda l, j: (l, 0, j)),
            pl.BlockSpec((None, 1, tn), lambda l, j: (l, 0, j)),
        ],
        out_specs=pl.BlockSpec((None, rows, tn), lambda l, j: (l, 0, j)),
        out_shape=jax.ShapeDtypeStruct((depth, rows, n), F32),
        compiler_params=_cparams(32, "parallel", "parallel"),
        name="modulation",
    )(cond, mod_w, mod_b.reshape(depth, 1, n))


def _inproj_kernel(x_ref, mod_ref, g_ref, w_ref, *out_refs, splits):
    x = x_ref[...]
    d = x.shape[-1]
    m = mod_ref[...]
    n = _rms(x, g_ref[...]) * (1.0 + m[..., d:2 * d]) + m[..., :d]
    if n.ndim == 3:
        n = jnp.swapaxes(n, 0, 1).reshape(-1, d)
    n = n.astype(BF16)
    off = 0
    for o_ref, width in zip(out_refs, splits):
        o_ref[...] = jnp.dot(n, w_ref[:, off:off + width], preferred_element_type=F32).astype(o_ref.dtype)
        off += width


def _inproj(x, x_spec, mod5, mod_spec, g3, w3, layer, wsel, splits, dtypes, rows, tm, grid):
    d, n = w3.shape[1:]
    return pl.pallas_call(
        functools.partial(_inproj_kernel, splits=splits),
        grid=(grid,),
        in_specs=[x_spec, mod_spec,
                  pl.BlockSpec((None, 1, d), lambda i: (layer, 0, 0)),
                  pl.BlockSpec((None, d, n), lambda i: (wsel, 0, 0))],
        out_specs=[pl.BlockSpec((tm, s), lambda i: (i, 0)) for s in splits],
        out_shape=[jax.ShapeDtypeStruct((rows, s), dt) for s, dt in zip(splits, dtypes)],
        compiler_params=_cparams(48, "parallel"),
        name="inproj",
    )(x, mod5, g3, w3)


def _softplus(x):
    return jnp.log1p(jnp.exp(-jnp.abs(x))) + jnp.maximum(x, 0.0)


def _chunk_order(j, n_ctx, n_all):
    return jnp.where(j < n_ctx, n_ctx - 1 - j, n_all - 1 - (j - n_ctx))


def _lru_kernel(xa_ref, ga_ref, cw_ref, cb_ref, wg_ref, bg_ref, lam_ref, o_ref,
                hf_ref, hb_ref, *ab_refs, n_ctx, n_all, ch):
    rows_total = xa_ref.shape[0]
    hd = xa_ref.shape[1]
    tb = ch // SUBLANE
    halo = 2 * SUBLANE
    row = lax.broadcasted_iota(jnp.int32, (ch, hd), 0)
    cw = cw_ref[...]
    cb = cb_ref[...]
    bufs = ((ab_refs[0:2], ab_refs[2:4]), (ab_refs[4:6], ab_refs[6:8]))
    dst = (hf_ref, hb_ref)
    order = (lambda j: j, lambda j: _chunk_order(j, n_ctx, n_all))
    k_log2a = [(-0.5 * LRU_C * LOG2E) * _softplus(-lam_ref[d]) for d in (0, 1)]

    def conv(c):
        r0 = pl.multiple_of(c * ch, ch)
        first = jnp.logical_or(c == 0, c == n_ctx)
        last = jnp.logical_or(c == n_ctx - 1, c == n_all - 1)
        rp = pl.multiple_of(jnp.maximum(r0 - halo, 0), halo)
        rn = pl.multiple_of(jnp.minimum(r0 + ch, rows_total - halo), halo)
        ext = jnp.concatenate([xa_ref[pl.ds(rp, halo), :], xa_ref[pl.ds(r0, ch), :],
                               xa_ref[pl.ds(rn, halo), :]], axis=0).astype(F32)
        xm2 = jnp.where(jnp.logical_and(first, row < 2 * SUBLANE), 0.0, ext[0:ch])
        xm1 = jnp.where(jnp.logical_and(first, row < SUBLANE), 0.0, ext[SUBLANE:SUBLANE + ch])
        x0 = ext[2 * SUBLANE:2 * SUBLANE + ch]
        xp1 = jnp.where(jnp.logical_and(last, row >= ch - SUBLANE), 0.0, ext[3 * SUBLANE:3 * SUBLANE + ch])
        return cb + cw[0:1] * xm2 + cw[1:2] * xm1 + cw[2:3] * x0 + cw[3:4] * xp1

    def coeffs(c, d, a_ref, b_ref):
        u = conv(c)
        t = jnp.tanh(jnp.dot(u.astype(BF16), wg_ref[d], preferred_element_type=F32) + bg_ref[d])
        a = jnp.exp2(t[:, :hd] * k_log2a[d] + k_log2a[d])
        a_ref[...] = a
        hu = 0.5 * u
        b_ref[...] = _sqrt_nonneg(1.0 - a * a) * (t[:, hd:] * hu + hu)

    def scan(c, d, a_ref, b_ref, h):
        r0 = pl.multiple_of(c * ch, ch)
        for t in range(tb):
            rr = (t if d == 0 else tb - 1 - t) * SUBLANE
            h = a_ref[rr:rr + SUBLANE, :] * h + b_ref[rr:rr + SUBLANE, :]
            dst[d][pl.ds(r0 + rr, SUBLANE), :] = h
        return h

    for d in (0, 1):
        coeffs(order[d](0), d, *bufs[d][0])

    def chunk_pair(i, hs):
        hs = list(hs)
        for slot in (0, 1):
            j = 2 * i + slot
            nxt = jnp.minimum(j + 1, n_all - 1)
            for d in (0, 1):
                coeffs(order[d](nxt), d, *bufs[d][1 - slot])
            for d in (0, 1):
                hs[d] = scan(order[d](j), d, *bufs[d][slot], hs[d])
        return tuple(hs)

    h0 = jnp.zeros((SUBLANE, hd), F32)
    lax.fori_loop(0, n_all // 2, chunk_pair, (h0, h0))

    def finish(c, carry):
        r0 = pl.multiple_of(c * ch, ch)
        g = ga_ref[pl.ds(r0, ch), :].astype(F32)
        o_ref[pl.ds(r0, ch), :] = ((hf_ref[pl.ds(r0, ch), :] + hb_ref[pl.ds(r0, ch), :]) * _silu(g)).astype(o_ref.dtype)
        return carry

    lax.fori_loop(0, n_all, finish, 0)


def _lru_mix(xa, ga, conv_w, conv_b, wg, bg, lam, e, n_ctx_rows):
    r, w = xa.shape
    hd = LRU_HEAD_DIM
    ch = SCAN_STEPS * SUBLANE
    assert (r // ch) % 2 == 0, "chunks are processed in pairs"
    kern = functools.partial(_lru_kernel, n_ctx=n_ctx_rows // ch, n_all=r // ch, ch=ch)
    return pl.pallas_call(
        kern,
        grid=(w // hd,),
        in_specs=[
            pl.BlockSpec((r, hd), lambda h: (0, h)),
            pl.BlockSpec((r, hd), lambda h: (0, h)),
            pl.BlockSpec((None, conv_w.shape[1], hd), lambda h: (e, 0, h)),
            pl.BlockSpec((None, 1, hd), lambda h: (e, 0, h)),
            pl.BlockSpec((None, 2, None, hd, 2 * hd), lambda h: (e, 0, h, 0, 0)),
            pl.BlockSpec((None, 2, 1, 2 * hd), lambda h: (e, 0, 0, h)),
            pl.BlockSpec((None, 2, 1, hd), lambda h: (e, 0, 0, h)),
        ],
        out_specs=pl.BlockSpec((r, hd), lambda h: (0, h)),
        out_shape=jax.ShapeDtypeStruct((r, w), BF16),
        scratch_shapes=[pltpu.VMEM((r, hd), F32)] * 2 + [pltpu.VMEM((ch, hd), F32)] * 8,
        compiler_params=_cparams(56, "parallel"),
        name="lru_mix",
    )(xa, ga, conv_w, conv_b, wg, bg, lam)


def _s5_disc_kernel(lre_ref, lim_ref, ldt_ref, bre_ref, bim_ref, are_ref, aim_ref, ore_ref, oim_ref):
    lam_re = lre_ref[...]
    lam_im = lim_ref[...]
    dt = jnp.exp(ldt_ref[...])
    mag = jnp.exp(lam_re * dt)
    ab_re = mag * jnp.cos(lam_im * dt)
    ab_im = mag * jnp.sin(lam_im * dt)
    den = lam_re * lam_re + lam_im * lam_im
    nr = ab_re - 1.0
    f_re = (nr * lam_re + ab_im * lam_im) / den
    f_im = (ab_im * lam_re - nr * lam_im) / den
    b_re = bre_ref[...]
    b_im = bim_ref[...]
    are_ref[...] = ab_re
    aim_ref[...] = ab_im
    ore_ref[...] = f_re * b_re - f_im * b_im
    oim_ref[...] = f_re * b_im + f_im * b_re


def _s5_discretise(lam_re, lam_im, log_dt, b_re, b_im):
    lead = b_re.shape[:-2]
    p, h = b_re.shape[-2:]
    nrow = math.prod(lead)
    flat = lambda v: jnp.broadcast_to(v[..., None], b_re.shape).reshape(nrow, p * h)
    shp = jax.ShapeDtypeStruct((nrow, p * h), F32)
    a_re, a_im, o_re, o_im = pl.pallas_call(
        _s5_disc_kernel, out_shape=[shp] * 4, name="s5_discretise",
    )(flat(lam_re), flat(lam_im), flat(log_dt), b_re.reshape(nrow, p * h), b_im.reshape(nrow, p * h))
    unflat = lambda v: v.reshape(*lead, p, h)
    return unflat(a_re)[..., 0], unflat(a_im)[..., 0], unflat(o_re), unflat(o_im)


def _s5_kernel(u_ref, bm_ref, cm_ref, ar_ref, ai_ref, dsk_ref, o_ref, yf_ref, yb_ref, *v_refs, n_ctx, n_all, ch):
    ns = ar_ref.shape[-1]
    tb = ch // SUBLANE
    bufs = (v_refs[0:2], v_refs[2:4])
    h_refs = (v_refs[4:6], v_refs[6:8])
    y_refs = (yf_ref, yb_ref)
    order = (lambda j: j, lambda j: _chunk_order(j, n_ctx, n_all))
    row0 = lambda j, d: pl.multiple_of(order[d](j) * ch, ch)

    def expand(j, slot):
        for d in (0, 1):
            bufs[d][slot][...] = jnp.dot(u_ref[pl.ds(row0(j, d), ch), :], bm_ref[d], preferred_element_type=F32)

    def scan(slot, hs):
        hs = [list(hs[0:2]), list(hs[2:4])]
        ar = [jnp.broadcast_to(ar_ref[d], (SUBLANE, ns)) for d in (0, 1)]
        ai = [jnp.broadcast_to(ai_ref[d], (SUBLANE, ns)) for d in (0, 1)]
        for k in range(tb // 2):
            for d in (0, 1):
                v_ref = bufs[d][slot]
                pair = (2 * k, 2 * k + 1) if d == 0 else (tb - 1 - 2 * k, tb - 2 - 2 * k)
                new = {}
                for t in pair:
                    rr = t * SUBLANE
                    hr, hi = hs[d]
                    nr = ar[d] * hr - ai[d] * hi + v_ref[rr:rr + SUBLANE, 0:ns]
                    ni = ar[d] * hi + ai[d] * hr + v_ref[rr:rr + SUBLANE, ns:2 * ns]
                    hs[d] = new[t] = [nr, ni]
                lo = min(pair)
                rows = slice(lo * SUBLANE, (lo + 2) * SUBLANE)
                for part in (0, 1):
                    both = jnp.concatenate([new[lo][part], new[lo + 1][part]], axis=0)
                    h_refs[d][slot][rows, part * ns:(part + 1) * ns] = both.astype(BF16)
        return (*hs[0], *hs[1])

    def project(j, slot):
        for d in (0, 1):
            y_refs[d][pl.ds(row0(j, d), ch), :] = jnp.dot(h_refs[d][slot][...], cm_ref[d], preferred_element_type=F32)

    last = n_all - 1
    z = jnp.zeros((SUBLANE, ns), F32)
    expand(0, 0)
    hs = scan(0, (z, z, z, z))
    expand(1, 1)

    def chunk_pair(i, hs):
        j = 2 * i
        project(j, 0)
        hs = scan(1, hs)
        expand(jnp.minimum(j + 2, last), 0)
        project(j + 1, 1)
        hs = scan(0, hs)
        expand(jnp.minimum(j + 3, last), 1)
        return hs

    lax.fori_loop(0, n_all // 2, chunk_pair, hs)

    def finish(c, carry):
        r0 = pl.multiple_of(c * ch, ch)
        u = u_ref[pl.ds(r0, ch), :].astype(F32)
        y = yf_ref[pl.ds(r0, ch), :] + yb_ref[pl.ds(r0, ch), :] + dsk_ref[...] * u
        o_ref[pl.ds(r0, ch), :] = jax.nn.gelu(y).astype(o_ref.dtype)
        return carry

    lax.fori_loop(0, n_all, finish, 0)


def _s5_mix(ub, bm, cm, ar, ai, dsk, e, n_ctx_rows):
    r, w = ub.shape
    ns = ar.shape[-1]
    ch = SCAN_STEPS * SUBLANE
    assert (r // ch) % 2 == 0, "chunks are processed in pairs"
    kern = functools.partial(_s5_kernel, n_ctx=n_ctx_rows // ch, n_all=r // ch, ch=ch)
    return pl.pallas_call(
        kern,
        grid=(w // LANE,),
        in_specs=[
            pl.BlockSpec((r, LANE), lambda s: (0, s)),
            pl.BlockSpec((None, 2, None, LANE, 2 * ns), lambda s: (e, 0, s, 0, 0)),
            pl.BlockSpec((None, 2, None, 2 * ns, LANE), lambda s: (e, 0, s, 0, 0)),
            pl.BlockSpec((None, 2, None, 1, ns), lambda s: (e, 0, s, 0, 0)),
            pl.BlockSpec((None, 2, None, 1, ns), lambda s: (e, 0, s, 0, 0)),
            pl.BlockSpec((None, 1, LANE), lambda s: (e, 0, s)),
        ],
        out_specs=pl.BlockSpec((r, LANE), lambda s: (0, s)),
        out_shape=jax.ShapeDtypeStruct((r, w), BF16),
        scratch_shapes=([pltpu.VMEM((r, LANE), F32)] * 2 + [pltpu.VMEM((ch, 2 * ns), F32)] * 4
                        + [pltpu.VMEM((ch, 2 * ns), BF16)] * 4),
        compiler_params=_cparams(54, "parallel"),
        name="s5_mix",
    )(ub, bm, cm, ar, ai, dsk)


def _outproj_even_kernel(ya_ref, yg_ref, gb_ref, x_ref, mod_ref, gw_ref, gbias_ref, w1_ref, w2_ref, o_ref):
    bsz, ts, d = x_ref.shape
    yg = yg_ref[...]
    z = jnp.dot(yg, gw_ref[...], preferred_element_type=F32) + gbias_ref[...]
    yb = yg.astype(F32) * _sigmoid(z) * _silu(gb_ref[...].astype(F32))
    o = jnp.dot(ya_ref[...], w1_ref[...], preferred_element_type=F32)
    o = o + jnp.dot(yb.astype(BF16), w2_ref[...], preferred_element_type=F32)
    o = jnp.swapaxes(o.reshape(ts, bsz, d), 0, 1)
    o_ref[...] = x_ref[...] + mod_ref[...][..., 2 * d:] * o


def _outproj_even(ya, yg, gb, xs, mod5, layer, seg, glu_w, glu_b, w_out, e, ts):
    bsz, steps, d = xs.shape
    wa, wb = ya.shape[1], yg.shape[1]
    tm = ts * bsz
    rows = lambda n: pl.BlockSpec((tm, n), lambda i: (i, 0))
    xblk = pl.BlockSpec((bsz, ts, d), lambda i: (0, i, 0))
    return pl.pallas_call(
        _outproj_even_kernel,
        grid=(steps // ts,),
        in_specs=[rows(wa), rows(wb), rows(wb), xblk,
                  pl.BlockSpec((None, None, bsz, 1, 3 * d), lambda i: (layer, seg(i), 0, 0, 0)),
                  pl.BlockSpec((None, wb, wb), lambda i: (e, 0, 0)),
                  pl.BlockSpec((None, 1, wb), lambda i: (e, 0, 0)),
                  pl.BlockSpec((None, wa, d), lambda i: (e, 0, 0)),
                  pl.BlockSpec((None, wb, d), lambda i: (e, wa // wb, 0))],
        out_specs=xblk,
        out_shape=jax.ShapeDtypeStruct(xs.shape, F32),
        compiler_params=_cparams(40, "parallel"),
        name="outproj_even",
    )(ya, yg, gb, xs, mod5, glu_w, glu_b, w_out, w_out)


def _outproj_odd_kernel(o_ref, x_ref, mod_ref, w_ref, *rest, final):
    d = x_ref.shape[1]
    y = jnp.dot(o_ref[...], w_ref[...], preferred_element_type=F32)
    x = x_ref[...] + mod_ref[...][:, 2 * d:] * y
    if final:
        g_ref, out_ref = rest
        x = _rms(x, g_ref[...])
    else:
        (out_ref,) = rest
    out_ref[...] = x


def _outproj_odd(att, xs2, mod5, layer, w, o, tm, blocks_per_sample, bsz, final_g=None):
    r, d = xs2.shape
    wspec = pl.BlockSpec((None, w.shape[1], d), lambda *_: (o, 0, 0))
    if final_g is None:
        blk = lambda n: pl.BlockSpec((tm, n), lambda i: (i, 0))
        mod_spec = pl.BlockSpec((None, None, None, 1, 3 * d),
                                lambda i: (layer, (i % blocks_per_sample == 0).astype(jnp.int32), i // blocks_per_sample, 0, 0))
        return pl.pallas_call(
            functools.partial(_outproj_odd_kernel, final=False),
            grid=(r // tm,),
            in_specs=[blk(att.shape[1]), blk(d), mod_spec, wspec],
            out_specs=blk(d),
            out_shape=jax.ShapeDtypeStruct((r, d), F32),
            compiler_params=_cparams(32, "parallel"),
            name="outproj_odd",
        )(att, xs2, mod5, w)
    nlat = blocks_per_sample - 1
    blk = lambda n: pl.BlockSpec((tm, n), lambda b, j: (b * blocks_per_sample + 1 + j, 0))
    return pl.pallas_call(
        functools.partial(_outproj_odd_kernel, final=True),
        grid=(bsz, nlat),
        in_specs=[blk(att.shape[1]), blk(d),
                  pl.BlockSpec((None, None, None, 1, 3 * d), lambda b, j: (layer, 0, b, 0, 0)),
                  wspec, pl.BlockSpec((1, d), lambda b, j: (0, 0))],
        out_specs=pl.BlockSpec((None, tm, d), lambda b, j: (b, j, 0)),
        out_shape=jax.ShapeDtypeStruct((bsz, nlat * tm, d), F32),
        compiler_params=_cparams(32, "parallel", "parallel"),
        name="outproj_final",
    )(att, xs2, mod5, w, final_g.reshape(1, d))


def _rope_fold(v, tab):
    w = v * tab
    return w + pltpu.roll(w, LANE // 2, axis=1)


def _mla_in_kernel(x_ref, mod_ref, g_ref, w_ref, tab_ref, qg_ref, kvg_ref, wq_ref, wkv_ref,
                   gate_ref, q_ref, k_ref, v_ref, *, q_rank, kv_rank):
    heads = q_ref.shape[0]
    d = x_ref.shape[1]
    m = mod_ref[...]
    n = (_rms(x_ref[...], g_ref[...]) * (1.0 + m[:, d:2 * d]) + m[:, :d]).astype(BF16)
    proj = lambda lo, hi: jnp.dot(n, w_ref[:, lo:hi], preferred_element_type=F32)
    off_g = q_rank + kv_rank
    off_kr = off_g + gate_ref.shape[1]
    gate_ref[...] = proj(off_g, off_kr).astype(gate_ref.dtype)
    tab = tab_ref[...]
    q = jnp.dot(_rms(proj(0, q_rank), qg_ref[...]).astype(BF16), wq_ref[...], preferred_element_type=F32)
    kv = jnp.dot(_rms(proj(q_rank, off_g), kvg_ref[...]).astype(BF16), wkv_ref[...], preferred_element_type=F32)
    kr = _rope_fold(proj(off_kr, off_kr + 2 * MLA_ROPE), tab)[:, :MLA_ROPE].astype(BF16)
    per = MLA_NOPE + 2 * MLA_ROPE
    for h in range(heads):
        qh = q[:, h * per:(h + 1) * per]
        q_ref[h, :, 0:MLA_NOPE] = (qh[:, :MLA_NOPE] * Q_PRESCALE).astype(BF16)
        qr = _rope_fold(qh[:, MLA_NOPE:], tab)[:, :MLA_ROPE] * Q_PRESCALE
        q_ref[h, :, MLA_NOPE:MLA_NOPE + MLA_ROPE] = qr.astype(BF16)
        kvh = kv[:, h * (MLA_NOPE + MLA_V):(h + 1) * (MLA_NOPE + MLA_V)]
        k_ref[h, :, 0:MLA_NOPE] = kvh[:, :MLA_NOPE].astype(BF16)
        k_ref[h, :, MLA_NOPE:MLA_NOPE + MLA_ROPE] = kr
        v_ref[h] = kvh[:, MLA_NOPE:].astype(BF16)


def _mla_in(xs2, mod5, norm_g3, w_in, tab, q_norm, kv_norm, wq, wkv, layer, o, tm, blocks_per_sample):
    r, d = xs2.shape
    heads = MLA_HEADS
    dk = MLA_NOPE + MLA_ROPE
    q_rank, kv_rank = q_norm.shape[-1], kv_norm.shape[-1]
    rows = lambda n: pl.BlockSpec((tm, n), lambda i: (i, 0))
    sel = lambda a, idx: pl.BlockSpec((None,) + a.shape[1:], lambda i: (idx, 0, 0))
    hrows = lambda n: pl.BlockSpec((heads, tm, n), lambda i: (0, i, 0))
    mod_spec = pl.BlockSpec(
        (None, None, None, 1, 3 * d),
        lambda i: (layer, (i % blocks_per_sample == 0).astype(jnp.int32), i // blocks_per_sample, 0, 0))
    return pl.pallas_call(
        functools.partial(_mla_in_kernel, q_rank=q_rank, kv_rank=kv_rank),
        grid=(r // tm,),
        in_specs=[rows(d), mod_spec, sel(norm_g3, layer), sel(w_in, o),
                  pl.BlockSpec((tm, LANE), lambda i: (i % blocks_per_sample, 0)),
                  sel(q_norm, o), sel(kv_norm, o), sel(wq, o), sel(wkv, o)],
        out_specs=[rows(heads * MLA_V), hrows(dk), hrows(dk), hrows(MLA_V)],
        out_shape=[jax.ShapeDtypeStruct((r, heads * MLA_V), BF16),
                   jax.ShapeDtypeStruct((heads, r, dk), BF16), jax.ShapeDtypeStruct((heads, r, dk), BF16),
                   jax.ShapeDtypeStruct((heads, r, MLA_V), BF16)],
        compiler_params=_cparams(48, "parallel"),
        name="mla_in",
    )(xs2, mod5, norm_g3, w_in, tab, q_norm, kv_norm, wq, wkv)


def _attn_kernel(q_ref, k_ref, v_ref, g_ref, o_ref, vx_ref, s0_ref, s1_ref, *, with_ctx, n_ctx, tq):
    per = k_ref.shape[0]
    n_lat = (per - n_ctx) // tq
    vx_ref[:, 0:MLA_V] = v_ref[...]
    vx_ref[:, MLA_V:] = jnp.ones((per, MLA_V), BF16)

    def scores(r0, nq, s_ref, nk):
        s_ref[0:nq, 0:nk] = lax.dot_general(q_ref[r0:r0 + nq, :], k_ref[0:nk, :], (((1,), (1,)), ((), ())),
                                            preferred_element_type=F32)

    def finish(r0, nq, s_ref, nk):
        s = s_ref[0:nq, 0:nk]
        p = jnp.exp2(s - jnp.max(s, axis=-1, keepdims=True))
        ox = jnp.dot(p.astype(BF16), vx_ref[0:nk, :], preferred_element_type=F32)
        o = ox[:, :MLA_V] / ox[:, MLA_V:]
        g = g_ref[r0:r0 + nq, :].astype(F32)
        o_ref[r0:r0 + nq, :] = (o * _silu(g)).astype(o_ref.dtype)

    if with_ctx:
        scores(0, n_ctx, s0_ref, n_ctx)
        finish(0, n_ctx, s0_ref, n_ctx)
    else:
        o_ref[0:n_ctx, :] = jnp.zeros((n_ctx, MLA_V), o_ref.dtype)

    s_refs = (s0_ref, s1_ref)
    scores(n_ctx, tq, s0_ref, per)
    for j in range(n_lat):
        if j + 1 < n_lat:
            scores(n_ctx + (j + 1) * tq, tq, s_refs[(j + 1) % 2], per)
        finish(n_ctx + j * tq, tq, s_refs[j % 2], per)


def _attention(q, k, v, gate, bsz, n_ctx, with_ctx):
    heads, r, dk = q.shape
    per = r // bsz
    tq = min(ATTN_Q_ROWS, per - n_ctx)
    assert (per - n_ctx) % tq == 0 and n_ctx <= tq
    blk = lambda n: pl.BlockSpec((None, per, n), lambda b, h: (h, b, 0))
    return pl.pallas_call(
        functools.partial(_attn_kernel, with_ctx=with_ctx, n_ctx=n_ctx, tq=tq),
        grid=(bsz, heads),
        in_specs=[blk(dk), blk(dk), blk(MLA_V), pl.BlockSpec((per, MLA_V), lambda b, h: (b, h))],
        out_specs=pl.BlockSpec((per, MLA_V), lambda b, h: (b, h)),
        out_shape=jax.ShapeDtypeStruct((r, heads * MLA_V), BF16),
        scratch_shapes=[pltpu.VMEM((per, 2 * MLA_V), BF16), pltpu.VMEM((tq, per), F32), pltpu.VMEM((tq, per), F32)],
        compiler_params=_cparams(40, "parallel", "parallel"),
        name="mla_attention",
    )(q, k, v, gate)


def _s5_matrices(a_re, a_im, bb_re, bb_im, c_re, c_im):
    gps = LANE // S5_GROUP
    ne, nd, g, p, h = bb_re.shape
    slabs = g // gps
    eye = jnp.eye(gps, dtype=F32)
    bd_in = lambda m: (jnp.swapaxes(m.reshape(ne, nd, slabs, gps, p, h), -1, -2)[..., None, :]
                       * eye[:, None, :, None]).reshape(ne, nd, slabs, gps * h, gps * p)
    bd_out = lambda m: (jnp.transpose(m.reshape(ne, nd, slabs, gps, h, p), (0, 1, 2, 5, 3, 4))[:, :, :, None]
                        * eye[:, None, :, None]).reshape(ne, nd, slabs, gps * p, gps * h)
    bm = jnp.concatenate([bd_in(bb_re), bd_in(bb_im)], axis=-1).astype(BF16)
    cm = jnp.concatenate([bd_out(c_re), -bd_out(c_im)], axis=-2).astype(BF16)
    ar = a_re.reshape(ne, nd, slabs, 1, gps * p)
    ai = a_im.reshape(ne, nd, slabs, 1, gps * p)
    return bm, cm, ar, ai


def _rope_table(n_ctx, n_lat):
    axis = MLA_ROPE // 2
    rows = n_lat // GRID_W
    row = jnp.repeat(jnp.arange(rows, dtype=F32), GRID_W)
    col = jnp.tile(jnp.arange(GRID_W, dtype=F32), rows)
    inv = ROPE_BASE ** (-jnp.arange(0, axis, 2, dtype=F32) / axis)
    ang = jnp.concatenate([row[:, None] * inv, col[:, None] * inv], axis=-1)
    cos, sin = jnp.cos(ang), jnp.sin(ang)
    lat = jnp.concatenate([cos, cos, -sin, sin], axis=-1)
    one, zero = jnp.ones((n_ctx, axis), F32), jnp.zeros((n_ctx, axis), F32)
    return jnp.concatenate([jnp.concatenate([one, one, zero, zero], axis=-1), lat], axis=0)


def _swap_halves(w):
    half = w.shape[-1] // 2
    return jnp.concatenate([w[..., half:], w[..., :half]], axis=-1)


def kernel(x, c, ctx, c_ctx, norm_g, mod_w, mod_b, ev_w_in, lru_conv_w, lru_conv_b, lru_wr, lru_br, lru_wi, lru_bi, lru_lam, s5_lam_re, s5_lam_im, s5_log_dt, s5_b_re, s5_b_im, s5_c_re, s5_c_im, s5_d, s5_glu_w, s5_glu_b, ev_w_out, mla_w_in, mla_q_norm, mla_w_uq, mla_kv_norm, mla_w_ukv, mla_w_out, final_g):
    bsz, n_lat, d = x.shape
    n_ctx = ctx.shape[1]
    steps = n_ctx + n_lat
    depth = mod_w.shape[0]
    n_even, n_odd = ev_w_in.shape[0], mla_w_in.shape[0]
    assert bsz == SUBLANE, "time-major layout maps the batch onto the sublanes of one vreg"
    assert depth % 2 == 0, "the last layer is an attention layer"
    lru_w = lru_conv_w.shape[-1]
    s5_w = s5_d.shape[-2] * s5_d.shape[-1]
    nh = lru_w // LRU_HEAD_DIM
    heads = MLA_HEADS
    q_rank, kv_rank = mla_q_norm.shape[-1], mla_kv_norm.shape[-1]

    cond = jnp.concatenate([c, jnp.broadcast_to(c_ctx[None], (bsz, d))], axis=0)
    mod5 = _modulation(cond, mod_w, mod_b).reshape(depth, 2, bsz, 1, 3 * d)
    norm_g3 = norm_g.reshape(depth, 1, d)

    ev_w_in_b = ev_w_in.astype(BF16)
    ev_w_out_b = ev_w_out.astype(BF16)
    glu_w_b = s5_glu_w.astype(BF16)
    glu_b3 = s5_glu_b.reshape(n_even, 1, s5_w)
    conv_b3 = lru_conv_b.reshape(n_even, 1, lru_w)
    wg = (0.5 * jnp.concatenate([lru_wr, lru_wi], axis=-1)).astype(BF16)
    bg = 0.5 * jnp.concatenate([lru_br.reshape(n_even, 2, nh, LRU_HEAD_DIM), lru_bi.reshape(n_even, 2, nh, LRU_HEAD_DIM)],
                               axis=-1).reshape(n_even, 2, 1, 2 * lru_w)
    lam4 = lru_lam.reshape(n_even, 2, 1, lru_w)
    a_re, a_im, bb_re, bb_im = _s5_discretise(s5_lam_re, s5_lam_im, s5_log_dt, s5_b_re, s5_b_im)
    s5_bm, s5_cm, s5_ar, s5_ai = _s5_matrices(a_re, a_im, bb_re, bb_im, s5_c_re, s5_c_im)
    dsk3 = s5_d.reshape(n_even, 1, s5_w)

    off_kr, off_g = q_rank + kv_rank, q_rank + kv_rank + MLA_ROPE
    w_kr = mla_w_in[..., off_kr:off_g]
    mla_w_in_b = jnp.concatenate([mla_w_in[..., :off_kr], mla_w_in[..., off_g:], w_kr, _swap_halves(w_kr)],
                                 axis=-1).astype(BF16)
    wq = mla_w_uq.reshape(n_odd, q_rank, heads, MLA_NOPE + MLA_ROPE)
    wq_b = jnp.concatenate([wq, _swap_halves(wq[..., MLA_NOPE:])], axis=-1).reshape(n_odd, q_rank, -1).astype(BF16)
    wkv_b = mla_w_ukv.astype(BF16)
    mla_w_out_b = mla_w_out.astype(BF16)
    q_norm3 = mla_q_norm.reshape(n_odd, 1, q_rank)
    kv_norm3 = mla_kv_norm.reshape(n_odd, 1, kv_rank)
    rope_tab = _rope_table(n_ctx, n_lat)

    ts = PROJ_STEPS
    ctx_blocks = n_ctx // ts
    seg_tm = lambda i: (i < ctx_blocks).astype(jnp.int32)
    tq = n_ctx
    blocks_per_sample = steps // tq
    rows = bsz * steps

    xs = jnp.concatenate([ctx, x], axis=1)
    for l in range(depth):
        if l % 2 == 0:
            e = l // 2
            x_spec = pl.BlockSpec((bsz, ts, d), lambda i: (0, i, 0))
            mod_spec = pl.BlockSpec((None, None, bsz, 1, 3 * d), lambda i: (l, seg_tm(i), 0, 0, 0))
            xa, ga, ub, gb = _inproj(xs, x_spec, mod5, mod_spec, norm_g3, ev_w_in_b, l, e,
                                     (lru_w, lru_w, s5_w, s5_w), (BF16,) * 4, rows, ts * bsz, steps // ts)
            ya = _lru_mix(xa, ga, lru_conv_w, conv_b3, wg, bg, lam4, e, n_ctx * bsz)
            yg = _s5_mix(ub, s5_bm, s5_cm, s5_ar, s5_ai, dsk3, e, n_ctx * bsz)
            xs = _outproj_even(ya, yg, gb, xs, mod5, l, seg_tm, glu_w_b, glu_b3, ev_w_out_b, e, ts)
        else:
            o = l // 2
            with_ctx = l < depth - 1
            xs2 = xs.reshape(rows, d)
            gate, q, k, v = _mla_in(xs2, mod5, norm_g3, mla_w_in_b, rope_tab, q_norm3, kv_norm3, wq_b, wkv_b,
                                    l, o, tq, blocks_per_sample)
            att = _attention(q, k, v, gate, bsz, n_ctx, with_ctx)
            if with_ctx:
                xs = _outproj_odd(att, xs2, mod5, l, mla_w_out_b, o, tq, blocks_per_sample, bsz).reshape(bsz, steps, d)
            else:
                xs = _outproj_odd(att, xs2, mod5, l, mla_w_out_b, o, tq, blocks_per_sample, bsz, final_g=final_g)
    return xs
```

```python
import functools
import math

import jax
import jax.numpy as jnp
from jax import lax
from jax.experimental import pallas as pl
from jax.experimental.pallas import tpu as pltpu

F32 = jnp.float32
BF16 = jnp.bfloat16

NORM_EPS = 1e-6
LRU_C = 8.0
LRU_HEAD_DIM = 128
S5_GROUP = 16
MLA_HEADS = 8
MLA_NOPE = 128
MLA_ROPE = 64
MLA_V = 128
MLA_SCALE = 1.0 / math.sqrt(MLA_NOPE + MLA_ROPE)
LOG2E = math.log2(math.e)
Q_PRESCALE = MLA_SCALE * LOG2E
ROPE_BASE = 10000.0
GRID_W = 64

LANE = 128
SUBLANE = 8
SCAN_STEPS = 32
PROJ_STEPS = 128
ODD_ROWS = 768
ATTN_Q_ROWS = 512
MIB = 1024 * 1024


def _cparams(vmem_mib, *sem):
    return pltpu.CompilerParams(dimension_semantics=tuple(sem), vmem_limit_bytes=int(vmem_mib * MIB))


def _sigmoid(x):
    return 0.5 * jnp.tanh(0.5 * x) + 0.5


def _silu(x):
    return x * _sigmoid(x)


def _sqrt_nonneg(x):
    return jnp.where(x > 0.0, x * lax.rsqrt(x), 0.0)


def _rms(x, g):
    return x * lax.rsqrt(jnp.mean(x * x, axis=-1, keepdims=True) + NORM_EPS) * g


def _mod_kernel(c_ref, w_ref, b_ref, o_ref):
    a = _silu(c_ref[...])
    o_ref[...] = jnp.dot(a, w_ref[...], preferred_element_type=F32, precision=lax.Precision.HIGHEST) + b_ref[...]


def _modulation(cond, mod_w, mod_b):
    depth, d, n = mod_w.shape
    rows = cond.shape[0]
    tn = min(n, 1024)
    return pl.pallas_call(
        _mod_kernel,
        grid=(depth, n // tn),
        in_specs=[
            pl.BlockSpec((rows, d), lambda l, j: (0, 0)),
            pl.BlockSpec((None, d, tn), lambda l, j: (l, 0, j)),
            pl.BlockSpec((None, 1, tn), lambda l, j: (l, 0, j)),
        ],
        out_specs=pl.BlockSpec((None, rows, tn), lambda l, j: (l, 0, j)),
        out_shape=jax.ShapeDtypeStruct((depth, rows, n), F32),
        compiler_params=_cparams(32, "parallel", "parallel"),
        name="modulation",
    )(cond, mod_w, mod_b.reshape(depth, 1, n))


def _inproj_kernel(x_ref, mod_ref, g_ref, w_ref, *out_refs, splits):
    x = x_ref[...]
    d = x.shape[-1]
    m = mod_ref[...]
    n = _rms(x, g_ref[...]) * (1.0 + m[..., d:2 * d]) + m[..., :d]
    if n.ndim == 3:
        n = jnp.swapaxes(n, 0, 1).reshape(-1, d)
    n = n.astype(BF16)
    off = 0
    for o_ref, width in zip(out_refs, splits):
        o_ref[...] = jnp.dot(n, w_ref[:, off:off + width], preferred_element_type=F32).astype(o_ref.dtype)
        off += width


def _inproj(x, x_spec, mod5, mod_spec, g3, w3, layer, wsel, splits, dtypes, rows, tm, grid):
    d, n = w3.shape[1:]
    return pl.pallas_call(
        functools.partial(_inproj_kernel, splits=splits),
        grid=(grid,),
        in_specs=[x_spec, mod_spec,
                  pl.BlockSpec((None, 1, d), lambda i: (layer, 0, 0)),
                  pl.BlockSpec((None, d, n), lambda i: (wsel, 0, 0))],
        out_specs=[pl.BlockSpec((tm, s), lambda i: (i, 0)) for s in splits],
        out_shape=[jax.ShapeDtypeStruct((rows, s), dt) for s, dt in zip(splits, dtypes)],
        compiler_params=_cparams(58, "parallel"),
        name="inproj",
    )(x, mod5, g3, w3)


def _softplus(x):
    return jnp.log1p(jnp.exp(-jnp.abs(x))) + jnp.maximum(x, 0.0)


def _chunk_order(j, n_ctx, n_all):
    return jnp.where(j < n_ctx, n_ctx - 1 - j, n_all - 1 - (j - n_ctx))


def _lru_kernel(xa_ref, ga_ref, cw_ref, cb_ref, wg_ref, bg_ref, lam_ref, o_ref,
                hf_ref, hb_ref, *ab_refs, n_ctx, n_all, ch):
    rows_total = xa_ref.shape[0]
    hd = xa_ref.shape[1]
    tb = ch // SUBLANE
    halo = 2 * SUBLANE
    row = lax.broadcasted_iota(jnp.int32, (ch, hd), 0)
    cw = cw_ref[...]
    cb = cb_ref[...]
    bufs = ((ab_refs[0:2], ab_refs[2:4]), (ab_refs[4:6], ab_refs[6:8]))
    dst = (hf_ref, hb_ref)
    order = (lambda j: j, lambda j: _chunk_order(j, n_ctx, n_all))
    k_log2a = [(-0.5 * LRU_C * LOG2E) * _softplus(-lam_ref[d]) for d in (0, 1)]

    def conv(c):
        r0 = pl.multiple_of(c * ch, ch)
        first = jnp.logical_or(c == 0, c == n_ctx)
        last = jnp.logical_or(c == n_ctx - 1, c == n_all - 1)
        rp = pl.multiple_of(jnp.maximum(r0 - halo, 0), halo)
        rn = pl.multiple_of(jnp.minimum(r0 + ch, rows_total - halo), halo)
        ext = jnp.concatenate([xa_ref[pl.ds(rp, halo), :], xa_ref[pl.ds(r0, ch), :],
                               xa_ref[pl.ds(rn, halo), :]], axis=0).astype(F32)
        xm2 = jnp.where(jnp.logical_and(first, row < 2 * SUBLANE), 0.0, ext[0:ch])
        xm1 = jnp.where(jnp.logical_and(first, row < SUBLANE), 0.0, ext[SUBLANE:SUBLANE + ch])
        x0 = ext[2 * SUBLANE:2 * SUBLANE + ch]
        xp1 = jnp.where(jnp.logical_and(last, row >= ch - SUBLANE), 0.0, ext[3 * SUBLANE:3 * SUBLANE + ch])
        return cb + cw[0:1] * xm2 + cw[1:2] * xm1 + cw[2:3] * x0 + cw[3:4] * xp1

    def coeffs(c, d, a_ref, b_ref):
        u = conv(c)
        t = jnp.tanh(jnp.dot(u.astype(BF16), wg_ref[d], preferred_element_type=F32) + bg_ref[d])
        a = jnp.exp2(t[:, :hd] * k_log2a[d] + k_log2a[d])
        a_ref[...] = a
        hu = 0.5 * u
        b_ref[...] = _sqrt_nonneg(1.0 - a * a) * (t[:, hd:] * hu + hu)

    def scan(c, d, a_ref, b_ref, h):
        r0 = pl.multiple_of(c * ch, ch)
        for t in range(tb):
            rr = (t if d == 0 else tb - 1 - t) * SUBLANE
            h = a_ref[rr:rr + SUBLANE, :] * h + b_ref[rr:rr + SUBLANE, :]
            dst[d][pl.ds(r0 + rr, SUBLANE), :] = h
        return h

    for d in (0, 1):
        coeffs(order[d](0), d, *bufs[d][0])

    def chunk_pair(i, hs):
        hs = list(hs)
        for slot in (0, 1):
            j = 2 * i + slot
            nxt = jnp.minimum(j + 1, n_all - 1)
            for d in (0, 1):
                coeffs(order[d](nxt), d, *bufs[d][1 - slot])
            for d in (0, 1):
                hs[d] = scan(order[d](j), d, *bufs[d][slot], hs[d])
        return tuple(hs)

    h0 = jnp.zeros((SUBLANE, hd), F32)
    lax.fori_loop(0, n_all // 2, chunk_pair, (h0, h0))

    def finish(c, carry):
        r0 = pl.multiple_of(c * ch, ch)
        g = ga_ref[pl.ds(r0, ch), :].astype(F32)
        o_ref[pl.ds(r0, ch), :] = ((hf_ref[pl.ds(r0, ch), :] + hb_ref[pl.ds(r0, ch), :]) * _silu(g)).astype(o_ref.dtype)
        return carry

    lax.fori_loop(0, n_all, finish, 0)


def _lru_mix(xa, ga, conv_w, conv_b, wg, bg, lam, e, n_ctx_rows):
    r, w = xa.shape
    hd = LRU_HEAD_DIM
    ch = SCAN_STEPS * SUBLANE
    assert (r // ch) % 2 == 0, "chunks are processed in pairs"
    kern = functools.partial(_lru_kernel, n_ctx=n_ctx_rows // ch, n_all=r // ch, ch=ch)
    return pl.pallas_call(
        kern,
        grid=(w // hd,),
        in_specs=[
            pl.BlockSpec((r, hd), lambda h: (0, h)),
            pl.BlockSpec((r, hd), lambda h: (0, h)),
            pl.BlockSpec((None, conv_w.shape[1], hd), lambda h: (e, 0, h)),
            pl.BlockSpec((None, 1, hd), lambda h: (e, 0, h)),
            pl.BlockSpec((None, 2, None, hd, 2 * hd), lambda h: (e, 0, h, 0, 0)),
            pl.BlockSpec((None, 2, 1, 2 * hd), lambda h: (e, 0, 0, h)),
            pl.BlockSpec((None, 2, 1, hd), lambda h: (e, 0, 0, h)),
        ],
        out_specs=pl.BlockSpec((r, hd), lambda h: (0, h)),
        out_shape=jax.ShapeDtypeStruct((r, w), BF16),
        scratch_shapes=[pltpu.VMEM((r, hd), F32)] * 2 + [pltpu.VMEM((ch, hd), F32)] * 8,
        compiler_params=_cparams(56, "parallel"),
        name="lru_mix",
    )(xa, ga, conv_w, conv_b, wg, bg, lam)


def _s5_disc_kernel(lre_ref, lim_ref, ldt_ref, bre_ref, bim_ref, are_ref, aim_ref, ore_ref, oim_ref):
    lam_re = lre_ref[...]
    lam_im = lim_ref[...]
    dt = jnp.exp(ldt_ref[...])
    mag = jnp.exp(lam_re * dt)
    ab_re = mag * jnp.cos(lam_im * dt)
    ab_im = mag * jnp.sin(lam_im * dt)
    den = lam_re * lam_re + lam_im * lam_im
    nr = ab_re - 1.0
    f_re = (nr * lam_re + ab_im * lam_im) / den
    f_im = (ab_im * lam_re - nr * lam_im) / den
    b_re = bre_ref[...]
    b_im = bim_ref[...]
    are_ref[...] = ab_re
    aim_ref[...] = ab_im
    ore_ref[...] = f_re * b_re - f_im * b_im
    oim_ref[...] = f_re * b_im + f_im * b_re


def _s5_discretise(lam_re, lam_im, log_dt, b_re, b_im):
    lead = b_re.shape[:-2]
    p, h = b_re.shape[-2:]
    nrow = math.prod(lead)
    flat = lambda v: jnp.broadcast_to(v[..., None], b_re.shape).reshape(nrow, p * h)
    shp = jax.ShapeDtypeStruct((nrow, p * h), F32)
    a_re, a_im, o_re, o_im = pl.pallas_call(
        _s5_disc_kernel, out_shape=[shp] * 4, name="s5_discretise",
    )(flat(lam_re), flat(lam_im), flat(log_dt), b_re.reshape(nrow, p * h), b_im.reshape(nrow, p * h))
    unflat = lambda v: v.reshape(*lead, p, h)
    return unflat(a_re)[..., 0], unflat(a_im)[..., 0], unflat(o_re), unflat(o_im)


def _s5_kernel(u_ref, bm_ref, cm_ref, ar_ref, ai_ref, dsk_ref, o_ref, yf_ref, yb_ref, *v_refs, n_ctx, n_all, ch):
    ns = ar_ref.shape[-1]
    tb = ch // SUBLANE
    bufs = (v_refs[0:2], v_refs[2:4])
    h_refs = (v_refs[4:6], v_refs[6:8])
    y_refs = (yf_ref, yb_ref)
    order = (lambda j: j, lambda j: _chunk_order(j, n_ctx, n_all))
    row0 = lambda j, d: pl.multiple_of(order[d](j) * ch, ch)

    def expand(j, slot):
        for d in (0, 1):
            bufs[d][slot][...] = jnp.dot(u_ref[pl.ds(row0(j, d), ch), :], bm_ref[d], preferred_element_type=F32)

    def scan(slot, hs):
        hs = [list(hs[0:2]), list(hs[2:4])]
        ar = [jnp.broadcast_to(ar_ref[d], (SUBLANE, ns)) for d in (0, 1)]
        ai = [jnp.broadcast_to(ai_ref[d], (SUBLANE, ns)) for d in (0, 1)]
        for k in range(tb // 2):
            for d in (0, 1):
                v_ref = bufs[d][slot]
                pair = (2 * k, 2 * k + 1) if d == 0 else (tb - 1 - 2 * k, tb - 2 - 2 * k)
                new = {}
                for t in pair:
                    rr = t * SUBLANE
                    hr, hi = hs[d]
                    nr = ar[d] * hr - ai[d] * hi + v_ref[rr:rr + SUBLANE, 0:ns]
                    ni = ar[d] * hi + ai[d] * hr + v_ref[rr:rr + SUBLANE, ns:2 * ns]
                    hs[d] = new[t] = [nr, ni]
                lo = min(pair)
                rows = slice(lo * SUBLANE, (lo + 2) * SUBLANE)
                for part in (0, 1):
                    both = jnp.concatenate([new[lo][part], new[lo + 1][part]], axis=0)
                    h_refs[d][slot][rows, part * ns:(part + 1) * ns] = both.astype(BF16)
        return (*hs[0], *hs[1])

    def project(j, slot):
        for d in (0, 1):
            y_refs[d][pl.ds(row0(j, d), ch), :] = jnp.dot(h_refs[d][slot][...], cm_ref[d], preferred_element_type=F32)

    last = n_all - 1
    z = jnp.zeros((SUBLANE, ns), F32)
    expand(0, 0)
    hs = scan(0, (z, z, z, z))
    expand(1, 1)

    def chunk_pair(i, hs):
        j = 2 * i
        project(j, 0)
        hs = scan(1, hs)
        expand(jnp.minimum(j + 2, last), 0)
        project(j + 1, 1)
        hs = scan(0, hs)
        expand(jnp.minimum(j + 3, last), 1)
        return hs

    lax.fori_loop(0, n_all // 2, chunk_pair, hs)

    def finish(c, carry):
        r0 = pl.multiple_of(c * ch, ch)
        u = u_ref[pl.ds(r0, ch), :].astype(F32)
        y = yf_ref[pl.ds(r0, ch), :] + yb_ref[pl.ds(r0, ch), :] + dsk_ref[...] * u
        o_ref[pl.ds(r0, ch), :] = jax.nn.gelu(y).astype(o_ref.dtype)
        return carry

    lax.fori_loop(0, n_all, finish, 0)


def _s5_mix(ub, bm, cm, ar, ai, dsk, e, n_ctx_rows):
    r, w = ub.shape
    ns = ar.shape[-1]
    ch = SCAN_STEPS * SUBLANE
    assert (r // ch) % 2 == 0, "chunks are processed in pairs"
    kern = functools.partial(_s5_kernel, n_ctx=n_ctx_rows // ch, n_all=r // ch, ch=ch)
    return pl.pallas_call(
        kern,
        grid=(w // LANE,),
        in_specs=[
            pl.BlockSpec((r, LANE), lambda s: (0, s)),
            pl.BlockSpec((None, 2, None, LANE, 2 * ns), lambda s: (e, 0, s, 0, 0)),
            pl.BlockSpec((None, 2, None, 2 * ns, LANE), lambda s: (e, 0, s, 0, 0)),
            pl.BlockSpec((None, 2, None, 1, ns), lambda s: (e, 0, s, 0, 0)),
            pl.BlockSpec((None, 2, None, 1, ns), lambda s: (e, 0, s, 0, 0)),
            pl.BlockSpec((None, 1, LANE), lambda s: (e, 0, s)),
        ],
        out_specs=pl.BlockSpec((r, LANE), lambda s: (0, s)),
        out_shape=jax.ShapeDtypeStruct((r, w), BF16),
        scratch_shapes=([pltpu.VMEM((r, LANE), F32)] * 2 + [pltpu.VMEM((ch, 2 * ns), F32)] * 4
                        + [pltpu.VMEM((ch, 2 * ns), BF16)] * 4),
        compiler_params=_cparams(54, "parallel"),
        name="s5_mix",
    )(ub, bm, cm, ar, ai, dsk)


def _outproj_even_kernel(ya_ref, yg_ref, gb_ref, x_ref, mod_ref, gw_ref, gbias_ref, w1_ref, w2_ref, o_ref):
    bsz, ts, d = x_ref.shape
    yg = yg_ref[...]
    z = jnp.dot(yg, gw_ref[...], preferred_element_type=F32) + gbias_ref[...]
    yb = yg.astype(F32) * _sigmoid(z) * _silu(gb_ref[...].astype(F32))
    o = jnp.dot(ya_ref[...], w1_ref[...], preferred_element_type=F32)
    o = o + jnp.dot(yb.astype(BF16), w2_ref[...], preferred_element_type=F32)
    o = jnp.swapaxes(o.reshape(ts, bsz, d), 0, 1)
    o_ref[...] = x_ref[...] + mod_ref[...][..., 2 * d:] * o


def _outproj_even(ya, yg, gb, xs, mod5, layer, seg, glu_w, glu_b, w_out, e, ts):
    bsz, steps, d = xs.shape
    wa, wb = ya.shape[1], yg.shape[1]
    tm = ts * bsz
    rows = lambda n: pl.BlockSpec((tm, n), lambda i: (i, 0))
    xblk = pl.BlockSpec((bsz, ts, d), lambda i: (0, i, 0))
    return pl.pallas_call(
        _outproj_even_kernel,
        grid=(steps // ts,),
        in_specs=[rows(wa), rows(wb), rows(wb), xblk,
                  pl.BlockSpec((None, None, bsz, 1, 3 * d), lambda i: (layer, seg(i), 0, 0, 0)),
                  pl.BlockSpec((None, wb, wb), lambda i: (e, 0, 0)),
                  pl.BlockSpec((None, 1, wb), lambda i: (e, 0, 0)),
                  pl.BlockSpec((None, wa, d), lambda i: (e, 0, 0)),
                  pl.BlockSpec((None, wb, d), lambda i: (e, wa // wb, 0))],
        out_specs=xblk,
        out_shape=jax.ShapeDtypeStruct(xs.shape, F32),
        compiler_params=_cparams(58, "parallel"),
        name="outproj_even",
    )(ya, yg, gb, xs, mod5, glu_w, glu_b, w_out, w_out)


def _outproj_odd_kernel(o_ref, x_ref, mod_ref, w_ref, *rest, final, n_ctx=None, blocks_per_sample=None):
    d = x_ref.shape[1]
    y = jnp.dot(o_ref[...], w_ref[...], preferred_element_type=F32)
    if final:
        g_ref, out_ref = rest
        out_ref[...] = _rms(x_ref[...] + mod_ref[...][:, 2 * d:] * y, g_ref[...])
    else:
        modc_ref, out_ref = rest
        m_head, m_lat = _segment_mods(mod_ref, modc_ref, blocks_per_sample)
        out_ref[...] = x_ref[...] + _two_segments(y, n_ctx, lambda v: v * m_head[:, 2 * d:], lambda v: v * m_lat[:, 2 * d:])


def _outproj_odd(att, xs2, mod5, layer, w, o, tm, blocks_per_sample, bsz, n_ctx=None, final_g=None):
    r, d = xs2.shape
    wspec = pl.BlockSpec((None, w.shape[1], d), lambda *_: (o, 0, 0))
    if final_g is None:
        blk = lambda n: pl.BlockSpec((tm, n), lambda i: (i, 0))
        mod_seg = lambda seg: pl.BlockSpec((None, None, None, 1, 3 * d),
                                           lambda i: (layer, seg, i // blocks_per_sample, 0, 0))
        return pl.pallas_call(
            functools.partial(_outproj_odd_kernel, final=False, n_ctx=n_ctx, blocks_per_sample=blocks_per_sample),
            grid=(r // tm,),
            in_specs=[blk(att.shape[1]), blk(d), mod_seg(0), wspec, mod_seg(1)],
            out_specs=blk(d),
            out_shape=jax.ShapeDtypeStruct((r, d), F32),
            compiler_params=_cparams(40, "parallel"),
            name="outproj_odd",
        )(att, xs2, mod5, w, mod5)
    nlat = blocks_per_sample - 1
    blk = lambda n: pl.BlockSpec((tm, n), lambda b, j: (b * blocks_per_sample + 1 + j, 0))
    return pl.pallas_call(
        functools.partial(_outproj_odd_kernel, final=True),
        grid=(bsz, nlat),
        in_specs=[blk(att.shape[1]), blk(d),
                  pl.BlockSpec((None, None, None, 1, 3 * d), lambda b, j: (layer, 0, b, 0, 0)),
                  wspec, pl.BlockSpec((1, d), lambda b, j: (0, 0))],
        out_specs=pl.BlockSpec((None, tm, d), lambda b, j: (b, j, 0)),
        out_shape=jax.ShapeDtypeStruct((bsz, nlat * tm, d), F32),
        compiler_params=_cparams(32, "parallel", "parallel"),
        name="outproj_final",
    )(att, xs2, mod5, w, final_g.reshape(1, d))


def _rope_fold(v, tab):
    w = v * tab
    return w + pltpu.roll(w, LANE // 2, axis=1)


def _segment_mods(mod_lat_ref, mod_ctx_ref, blocks_per_sample):
    first = pl.program_id(0) % blocks_per_sample == 0
    m_lat = mod_lat_ref[...]
    return jnp.where(first, mod_ctx_ref[...], m_lat), m_lat


def _two_segments(y, n_head, f_head, f_rest):
    if n_head == y.shape[0]:
        return f_head(y)
    return jnp.concatenate([f_head(y[:n_head]), f_rest(y[n_head:])], axis=0)


def _mla_in_kernel(x_ref, mod_ref, modc_ref, g_ref, w_ref, tab_ref, qg_ref, kvg_ref, wq_ref, wkv_ref,
                   gate_ref, q_ref, k_ref, v_ref, *, q_rank, kv_rank, n_ctx, blocks_per_sample):
    heads = q_ref.shape[0]
    d = x_ref.shape[1]
    m_head, m_lat = _segment_mods(mod_ref, modc_ref, blocks_per_sample)
    y = _rms(x_ref[...], g_ref[...])
    shift_scale = lambda m: (lambda v: v * (1.0 + m[:, d:2 * d]) + m[:, :d])
    n = _two_segments(y, n_ctx, shift_scale(m_head), shift_scale(m_lat)).astype(BF16)
    proj = lambda lo, hi: jnp.dot(n, w_ref[:, lo:hi], preferred_element_type=F32)
    off_g = q_rank + kv_rank
    off_kr = off_g + gate_ref.shape[1]
    gate_ref[...] = proj(off_g, off_kr).astype(gate_ref.dtype)
    tab = tab_ref[...]
    q = jnp.dot(_rms(proj(0, q_rank), qg_ref[...]).astype(BF16), wq_ref[...], preferred_element_type=F32)
    kv = jnp.dot(_rms(proj(q_rank, off_g), kvg_ref[...]).astype(BF16), wkv_ref[...], preferred_element_type=F32)
    kr = _rope_fold(proj(off_kr, off_kr + 2 * MLA_ROPE), tab)[:, :MLA_ROPE].astype(BF16)
    per = MLA_NOPE + 2 * MLA_ROPE
    for h in range(heads):
        qh = q[:, h * per:(h + 1) * per]
        q_ref[h, :, 0:MLA_NOPE] = (qh[:, :MLA_NOPE] * Q_PRESCALE).astype(BF16)
        qr = _rope_fold(qh[:, MLA_NOPE:], tab)[:, :MLA_ROPE] * Q_PRESCALE
        q_ref[h, :, MLA_NOPE:MLA_NOPE + MLA_ROPE] = qr.astype(BF16)
        kvh = kv[:, h * (MLA_NOPE + MLA_V):(h + 1) * (MLA_NOPE + MLA_V)]
        k_ref[h, :, 0:MLA_NOPE] = kvh[:, :MLA_NOPE].astype(BF16)
        k_ref[h, :, MLA_NOPE:MLA_NOPE + MLA_ROPE] = kr
        v_ref[h] = kvh[:, MLA_NOPE:].astype(BF16)


def _mla_in(xs2, mod5, norm_g3, w_in, tab, q_norm, kv_norm, wq, wkv, layer, o, tm, blocks_per_sample, n_ctx):
    r, d = xs2.shape
    heads = MLA_HEADS
    dk = MLA_NOPE + MLA_ROPE
    q_rank, kv_rank = q_norm.shape[-1], kv_norm.shape[-1]
    rows = lambda n: pl.BlockSpec((tm, n), lambda i: (i, 0))
    sel = lambda a, idx: pl.BlockSpec((None,) + a.shape[1:], lambda i: (idx, 0, 0))
    hrows = lambda n: pl.BlockSpec((heads, tm, n), lambda i: (0, i, 0))
    mod_seg = lambda seg: pl.BlockSpec((None, None, None, 1, 3 * d),
                                       lambda i: (layer, seg, i // blocks_per_sample, 0, 0))
    return pl.pallas_call(
        functools.partial(_mla_in_kernel, q_rank=q_rank, kv_rank=kv_rank, n_ctx=n_ctx,
                          blocks_per_sample=blocks_per_sample),
        grid=(r // tm,),
        in_specs=[rows(d), mod_seg(0), mod_seg(1), sel(norm_g3, layer), sel(w_in, o),
                  pl.BlockSpec((tm, LANE), lambda i: (i % blocks_per_sample, 0)),
                  sel(q_norm, o), sel(kv_norm, o), sel(wq, o), sel(wkv, o)],
        out_specs=[rows(heads * MLA_V), hrows(dk), hrows(dk), hrows(MLA_V)],
        out_shape=[jax.ShapeDtypeStruct((r, heads * MLA_V), BF16),
                   jax.ShapeDtypeStruct((heads, r, dk), BF16), jax.ShapeDtypeStruct((heads, r, dk), BF16),
                   jax.ShapeDtypeStruct((heads, r, MLA_V), BF16)],
        compiler_params=_cparams(56, "parallel"),
        name="mla_in",
    )(xs2, mod5, mod5, norm_g3, w_in, tab, q_norm, kv_norm, wq, wkv)


def _attn_kernel(q_ref, k_ref, v_ref, g_ref, o_ref, vx_ref, s0_ref, s1_ref, *, with_ctx, n_ctx, tq):
    per = k_ref.shape[0]
    n_lat = (per - n_ctx) // tq
    vx_ref[:, 0:MLA_V] = v_ref[...]
    vx_ref[:, MLA_V:] = jnp.ones((per, MLA_V), BF16)

    def scores(r0, nq, s_ref, nk):
        s_ref[0:nq, 0:nk] = lax.dot_general(q_ref[r0:r0 + nq, :], k_ref[0:nk, :], (((1,), (1,)), ((), ())),
                                            preferred_element_type=F32)

    def finish(r0, nq, s_ref, nk):
        s = s_ref[0:nq, 0:nk]
        p = jnp.exp2(s - jnp.max(s, axis=-1, keepdims=True))
        ox = jnp.dot(p.astype(BF16), vx_ref[0:nk, :], preferred_element_type=F32)
        o = ox[:, :MLA_V] / ox[:, MLA_V:]
        g = g_ref[r0:r0 + nq, :].astype(F32)
        o_ref[r0:r0 + nq, :] = (o * _silu(g)).astype(o_ref.dtype)

    if with_ctx:
        scores(0, n_ctx, s0_ref, n_ctx)
        finish(0, n_ctx, s0_ref, n_ctx)
    else:
        o_ref[0:n_ctx, :] = jnp.zeros((n_ctx, MLA_V), o_ref.dtype)

    s_refs = (s0_ref, s1_ref)
    scores(n_ctx, tq, s0_ref, per)
    for j in range(n_lat):
        if j + 1 < n_lat:
            scores(n_ctx + (j + 1) * tq, tq, s_refs[(j + 1) % 2], per)
        finish(n_ctx + j * tq, tq, s_refs[j % 2], per)


def _attention(q, k, v, gate, bsz, n_ctx, with_ctx):
    heads, r, dk = q.shape
    per = r // bsz
    tq = min(ATTN_Q_ROWS, per - n_ctx)
    assert (per - n_ctx) % tq == 0 and n_ctx <= tq
    blk = lambda n: pl.BlockSpec((None, per, n), lambda b, h: (h, b, 0))
    return pl.pallas_call(
        functools.partial(_attn_kernel, with_ctx=with_ctx, n_ctx=n_ctx, tq=tq),
        grid=(bsz, heads),
        in_specs=[blk(dk), blk(dk), blk(MLA_V), pl.BlockSpec((per, MLA_V), lambda b, h: (b, h))],
        out_specs=pl.BlockSpec((per, MLA_V), lambda b, h: (b, h)),
        out_shape=jax.ShapeDtypeStruct((r, heads * MLA_V), BF16),
        scratch_shapes=[pltpu.VMEM((per, 2 * MLA_V), BF16), pltpu.VMEM((tq, per), F32), pltpu.VMEM((tq, per), F32)],
        compiler_params=_cparams(40, "parallel", "parallel"),
        name="mla_attention",
    )(q, k, v, gate)


def _s5_matrices(a_re, a_im, bb_re, bb_im, c_re, c_im):
    gps = LANE // S5_GROUP
    ne, nd, g, p, h = bb_re.shape
    slabs = g // gps
    eye = jnp.eye(gps, dtype=F32)
    bd_in = lambda m: (jnp.swapaxes(m.reshape(ne, nd, slabs, gps, p, h), -1, -2)[..., None, :]
                       * eye[:, None, :, None]).reshape(ne, nd, slabs, gps * h, gps * p)
    bd_out = lambda m: (jnp.transpose(m.reshape(ne, nd, slabs, gps, h, p), (0, 1, 2, 5, 3, 4))[:, :, :, None]
                        * eye[:, None, :, None]).reshape(ne, nd, slabs, gps * p, gps * h)
    bm = jnp.concatenate([bd_in(bb_re), bd_in(bb_im)], axis=-1).astype(BF16)
    cm = jnp.concatenate([bd_out(c_re), -bd_out(c_im)], axis=-2).astype(BF16)
    ar = a_re.reshape(ne, nd, slabs, 1, gps * p)
    ai = a_im.reshape(ne, nd, slabs, 1, gps * p)
    return bm, cm, ar, ai


def _rope_table(n_ctx, n_lat):
    axis = MLA_ROPE // 2
    rows = n_lat // GRID_W
    row = jnp.repeat(jnp.arange(rows, dtype=F32), GRID_W)
    col = jnp.tile(jnp.arange(GRID_W, dtype=F32), rows)
    inv = ROPE_BASE ** (-jnp.arange(0, axis, 2, dtype=F32) / axis)
    ang = jnp.concatenate([row[:, None] * inv, col[:, None] * inv], axis=-1)
    cos, sin = jnp.cos(ang), jnp.sin(ang)
    lat = jnp.concatenate([cos, cos, -sin, sin], axis=-1)
    one, zero = jnp.ones((n_ctx, axis), F32), jnp.zeros((n_ctx, axis), F32)
    return jnp.concatenate([jnp.concatenate([one, one, zero, zero], axis=-1), lat], axis=0)


def _swap_halves(w):
    half = w.shape[-1] // 2
    return jnp.concatenate([w[..., half:], w[..., :half]], axis=-1)


def kernel(x, c, ctx, c_ctx, norm_g, mod_w, mod_b, ev_w_in, lru_conv_w, lru_conv_b, lru_wr, lru_br, lru_wi, lru_bi, lru_lam, s5_lam_re, s5_lam_im, s5_log_dt, s5_b_re, s5_b_im, s5_c_re, s5_c_im, s5_d, s5_glu_w, s5_glu_b, ev_w_out, mla_w_in, mla_q_norm, mla_w_uq, mla_kv_norm, mla_w_ukv, mla_w_out, final_g):
    bsz, n_lat, d = x.shape
    n_ctx = ctx.shape[1]
    steps = n_ctx + n_lat
    depth = mod_w.shape[0]
    n_even, n_odd = ev_w_in.shape[0], mla_w_in.shape[0]
    assert bsz == SUBLANE, "time-major layout maps the batch onto the sublanes of one vreg"
    assert depth % 2 == 0, "the last layer is an attention layer"
    lru_w = lru_conv_w.shape[-1]
    s5_w = s5_d.shape[-2] * s5_d.shape[-1]
    nh = lru_w // LRU_HEAD_DIM
    heads = MLA_HEADS
    q_rank, kv_rank = mla_q_norm.shape[-1], mla_kv_norm.shape[-1]

    cond = jnp.concatenate([c, jnp.broadcast_to(c_ctx[None], (bsz, d))], axis=0)
    mod5 = _modulation(cond, mod_w, mod_b).reshape(depth, 2, bsz, 1, 3 * d)
    norm_g3 = norm_g.reshape(depth, 1, d)

    ev_w_in_b = ev_w_in.astype(BF16)
    ev_w_out_b = ev_w_out.astype(BF16)
    glu_w_b = s5_glu_w.astype(BF16)
    glu_b3 = s5_glu_b.reshape(n_even, 1, s5_w)
    conv_b3 = lru_conv_b.reshape(n_even, 1, lru_w)
    wg = (0.5 * jnp.concatenate([lru_wr, lru_wi], axis=-1)).astype(BF16)
    bg = 0.5 * jnp.concatenate([lru_br.reshape(n_even, 2, nh, LRU_HEAD_DIM), lru_bi.reshape(n_even, 2, nh, LRU_HEAD_DIM)],
                               axis=-1).reshape(n_even, 2, 1, 2 * lru_w)
    lam4 = lru_lam.reshape(n_even, 2, 1, lru_w)
    a_re, a_im, bb_re, bb_im = _s5_discretise(s5_lam_re, s5_lam_im, s5_log_dt, s5_b_re, s5_b_im)
    s5_bm, s5_cm, s5_ar, s5_ai = _s5_matrices(a_re, a_im, bb_re, bb_im, s5_c_re, s5_c_im)
    dsk3 = s5_d.reshape(n_even, 1, s5_w)

    off_kr, off_g = q_rank + kv_rank, q_rank + kv_rank + MLA_ROPE
    w_kr = mla_w_in[..., off_kr:off_g]
    mla_w_in_b = jnp.concatenate([mla_w_in[..., :off_kr], mla_w_in[..., off_g:], w_kr, _swap_halves(w_kr)],
                                 axis=-1).astype(BF16)
    wq = mla_w_uq.reshape(n_odd, q_rank, heads, MLA_NOPE + MLA_ROPE)
    wq_b = jnp.concatenate([wq, _swap_halves(wq[..., MLA_NOPE:])], axis=-1).reshape(n_odd, q_rank, -1).astype(BF16)
    wkv_b = mla_w_ukv.astype(BF16)
    mla_w_out_b = mla_w_out.astype(BF16)
    q_norm3 = mla_q_norm.reshape(n_odd, 1, q_rank)
    kv_norm3 = mla_kv_norm.reshape(n_odd, 1, kv_rank)
    rope_tab = _rope_table(n_ctx, n_lat)

    ts = PROJ_STEPS
    ctx_blocks = n_ctx // ts
    seg_tm = lambda i: (i < ctx_blocks).astype(jnp.int32)
    tq = n_ctx
    blocks_per_sample = steps // tq
    rows = bsz * steps
    tm_odd = ODD_ROWS if steps % ODD_ROWS == 0 and n_ctx < ODD_ROWS else tq
    assert steps % tm_odd == 0 and n_ctx <= tm_odd

    xs = jnp.concatenate([ctx, x], axis=1)
    for l in range(depth):
        if l % 2 == 0:
            e = l // 2
            x_spec = pl.BlockSpec((bsz, ts, d), lambda i: (0, i, 0))
            mod_spec = pl.BlockSpec((None, None, bsz, 1, 3 * d), lambda i: (l, seg_tm(i), 0, 0, 0))
            xa, ga, ub, gb = _inproj(xs, x_spec, mod5, mod_spec, norm_g3, ev_w_in_b, l, e,
                                     (lru_w, lru_w, s5_w, s5_w), (BF16,) * 4, rows, ts * bsz, steps // ts)
            ya = _lru_mix(xa, ga, lru_conv_w, conv_b3, wg, bg, lam4, e, n_ctx * bsz)
            yg = _s5_mix(ub, s5_bm, s5_cm, s5_ar, s5_ai, dsk3, e, n_ctx * bsz)
            xs = _outproj_even(ya, yg, gb, xs, mod5, l, seg_tm, glu_w_b, glu_b3, ev_w_out_b, e, ts)
        else:
            o = l // 2
            with_ctx = l < depth - 1
            xs2 = xs.reshape(rows, d)
            gate, q, k, v = _mla_in(xs2, mod5, norm_g3, mla_w_in_b, rope_tab, q_norm3, kv_norm3, wq_b, wkv_b,
                                    l, o, tm_odd, steps // tm_odd, n_ctx)
            att = _attention(q, k, v, gate, bsz, n_ctx, with_ctx)
            if with_ctx:
                xs = _outproj_odd(att, xs2, mod5, l, mla_w_out_b, o, tm_odd, steps // tm_odd, bsz,
                                  n_ctx=n_ctx).reshape(bsz, steps, d)
            else:
                xs = _outproj_odd(att, xs2, mod5, l, mla_w_out_b, o, tq, blocks_per_sample, bsz, final_g=final_g)
    return xs
```

```python
import functools
import math

import jax
import jax.numpy as jnp
from jax import lax
from jax.experimental import pallas as pl
from jax.experimental.pallas import tpu as pltpu

F32 = jnp.float32
BF16 = jnp.bfloat16

NORM_EPS = 1e-6
LRU_C = 8.0
LRU_HEAD_DIM = 128
S5_GROUP = 16
MLA_HEADS = 8
MLA_NOPE = 128
MLA_ROPE = 64
MLA_V = 128
MLA_SCALE = 1.0 / math.sqrt(MLA_NOPE + MLA_ROPE)
LOG2E = math.log2(math.e)
Q_PRESCALE = MLA_SCALE * LOG2E
ROPE_BASE = 10000.0
GRID_W = 64

LANE = 128
SUBLANE = 8
SCAN_STEPS = 32
PROJ_STEPS = 128
ODD_ROWS = 768
ATTN_Q_ROWS = 512
MIB = 1024 * 1024


def _cparams(vmem_mib, *sem):
    return pltpu.CompilerParams(dimension_semantics=tuple(sem), vmem_limit_bytes=int(vmem_mib * MIB))


def _sigmoid(x):
    return 0.5 * jnp.tanh(0.5 * x) + 0.5


def _silu(x):
    return x * _sigmoid(x)


def _sqrt_nonneg(x):
    return jnp.where(x > 0.0, x * lax.rsqrt(x), 0.0)


def _rms(x, g):
    return x * lax.rsqrt(jnp.mean(x * x, axis=-1, keepdims=True) + NORM_EPS) * g


def _mod_kernel(c_ref, w_ref, b_ref, o_ref):
    a = _silu(c_ref[...])
    o_ref[...] = jnp.dot(a, w_ref[...], preferred_element_type=F32, precision=lax.Precision.HIGHEST) + b_ref[...]


def _modulation(cond, mod_w, mod_b):
    depth, d, n = mod_w.shape
    rows = cond.shape[0]
    tn = min(n, 1024)
    return pl.pallas_call(
        _mod_kernel,
        grid=(depth, n // tn),
        in_specs=[
            pl.BlockSpec((rows, d), lambda l, j: (0, 0)),
            pl.BlockSpec((None, d, tn), lambda l, j: (l, 0, j)),
            pl.BlockSpec((None, 1, tn), lambda l, j: (l, 0, j)),
        ],
        out_specs=pl.BlockSpec((None, rows, tn), lambda l, j: (l, 0, j)),
        out_shape=jax.ShapeDtypeStruct((depth, rows, n), F32),
        compiler_params=_cparams(32, "parallel", "parallel"),
        name="modulation",
    )(cond, mod_w, mod_b.reshape(depth, 1, n))


def _residual_specs(xs, ts, ctx_blocks):
    bsz, _, d = xs[0].shape
    if len(xs) == 1:
        return [pl.BlockSpec((bsz, ts, d), lambda i: (0, i, 0))]
    return [pl.BlockSpec((bsz, ts, d), lambda i: (0, jnp.minimum(i, ctx_blocks - 1), 0)),
            pl.BlockSpec((bsz, ts, d), lambda i: (0, jnp.maximum(i - ctx_blocks, 0), 0))]


def _residual_rows(x_refs, ctx_blocks):
    if len(x_refs) == 1:
        return x_refs[0][...]
    return jnp.where(pl.program_id(0) < ctx_blocks, x_refs[0][...], x_refs[1][...])


def _inproj_kernel(*refs, splits, n_res, ctx_blocks):
    x_refs, (mod_ref, g_ref, w_ref), out_refs = refs[:n_res], refs[n_res:n_res + 3], refs[n_res + 3:]
    x = _residual_rows(x_refs, ctx_blocks)
    d = x.shape[-1]
    m = mod_ref[...]
    n = _rms(x, g_ref[...]) * (1.0 + m[..., d:2 * d]) + m[..., :d]
    n = jnp.swapaxes(n, 0, 1).reshape(-1, d).astype(BF16)
    off = 0
    for o_ref, width in zip(out_refs, splits):
        o_ref[...] = jnp.dot(n, w_ref[:, off:off + width], preferred_element_type=F32).astype(o_ref.dtype)
        off += width


def _inproj(xs, mod5, g3, w3, layer, e, seg, splits, dtypes, ts, ctx_blocks, steps):
    bsz, _, d = xs[0].shape
    n = w3.shape[2]
    tm = ts * bsz
    return pl.pallas_call(
        functools.partial(_inproj_kernel, splits=splits, n_res=len(xs), ctx_blocks=ctx_blocks),
        grid=(steps // ts,),
        in_specs=_residual_specs(xs, ts, ctx_blocks) + [
            pl.BlockSpec((None, None, bsz, 1, 3 * d), lambda i: (layer, seg(i), 0, 0, 0)),
            pl.BlockSpec((None, 1, d), lambda i: (layer, 0, 0)),
            pl.BlockSpec((None, d, n), lambda i: (e, 0, 0))],
        out_specs=[pl.BlockSpec((tm, s), lambda i: (i, 0)) for s in splits],
        out_shape=[jax.ShapeDtypeStruct((steps * bsz, s), dt) for s, dt in zip(splits, dtypes)],
        compiler_params=_cparams(58, "parallel"),
        name="inproj",
    )(*xs, mod5, g3, w3)


def _softplus(x):
    return jnp.log1p(jnp.exp(-jnp.abs(x))) + jnp.maximum(x, 0.0)


def _chunk_order(j, n_ctx, n_all):
    return jnp.where(j < n_ctx, n_ctx - 1 - j, n_all - 1 - (j - n_ctx))


def _lru_kernel(xa_ref, ga_ref, cw_ref, cb_ref, wg_ref, bg_ref, lam_ref, o_ref,
                hf_ref, hb_ref, *ab_refs, n_ctx, n_all, ch):
    rows_total = xa_ref.shape[0]
    hd = xa_ref.shape[1]
    tb = ch // SUBLANE
    halo = 2 * SUBLANE
    row = lax.broadcasted_iota(jnp.int32, (ch, hd), 0)
    cw = cw_ref[...]
    cb = cb_ref[...]
    bufs = ((ab_refs[0:2], ab_refs[2:4]), (ab_refs[4:6], ab_refs[6:8]))
    dst = (hf_ref, hb_ref)
    order = (lambda j: j, lambda j: _chunk_order(j, n_ctx, n_all))
    k_log2a = [(-0.5 * LRU_C * LOG2E) * _softplus(-lam_ref[d]) for d in (0, 1)]

    def conv(c):
        r0 = pl.multiple_of(c * ch, ch)
        first = jnp.logical_or(c == 0, c == n_ctx)
        last = jnp.logical_or(c == n_ctx - 1, c == n_all - 1)
        rp = pl.multiple_of(jnp.maximum(r0 - halo, 0), halo)
        rn = pl.multiple_of(jnp.minimum(r0 + ch, rows_total - halo), halo)
        ext = jnp.concatenate([xa_ref[pl.ds(rp, halo), :], xa_ref[pl.ds(r0, ch), :],
                               xa_ref[pl.ds(rn, halo), :]], axis=0).astype(F32)
        xm2 = jnp.where(jnp.logical_and(first, row < 2 * SUBLANE), 0.0, ext[0:ch])
        xm1 = jnp.where(jnp.logical_and(first, row < SUBLANE), 0.0, ext[SUBLANE:SUBLANE + ch])
        x0 = ext[2 * SUBLANE:2 * SUBLANE + ch]
        xp1 = jnp.where(jnp.logical_and(last, row >= ch - SUBLANE), 0.0, ext[3 * SUBLANE:3 * SUBLANE + ch])
        return cb + cw[0:1] * xm2 + cw[1:2] * xm1 + cw[2:3] * x0 + cw[3:4] * xp1

    def coeffs(c, d, a_ref, b_ref):
        u = conv(c)
        t = jnp.tanh(jnp.dot(u.astype(BF16), wg_ref[d], preferred_element_type=F32) + bg_ref[d])
        a = jnp.exp2(t[:, :hd] * k_log2a[d] + k_log2a[d])
        a_ref[...] = a
        hu = 0.5 * u
        b_ref[...] = _sqrt_nonneg(1.0 - a * a) * (t[:, hd:] * hu + hu)

    def scan(c, d, a_ref, b_ref, h):
        r0 = pl.multiple_of(c * ch, ch)
        for t in range(tb):
            rr = (t if d == 0 else tb - 1 - t) * SUBLANE
            h = a_ref[rr:rr + SUBLANE, :] * h + b_ref[rr:rr + SUBLANE, :]
            dst[d][pl.ds(r0 + rr, SUBLANE), :] = h
        return h

    for d in (0, 1):
        coeffs(order[d](0), d, *bufs[d][0])

    def chunk_pair(i, hs):
        hs = list(hs)
        for slot in (0, 1):
            j = 2 * i + slot
            nxt = jnp.minimum(j + 1, n_all - 1)
            for d in (0, 1):
                coeffs(order[d](nxt), d, *bufs[d][1 - slot])
            for d in (0, 1):
                hs[d] = scan(order[d](j), d, *bufs[d][slot], hs[d])
        return tuple(hs)

    h0 = jnp.zeros((SUBLANE, hd), F32)
    lax.fori_loop(0, n_all // 2, chunk_pair, (h0, h0))

    def finish(c, carry):
        r0 = pl.multiple_of(c * ch, ch)
        g = ga_ref[pl.ds(r0, ch), :].astype(F32)
        o_ref[pl.ds(r0, ch), :] = ((hf_ref[pl.ds(r0, ch), :] + hb_ref[pl.ds(r0, ch), :]) * _silu(g)).astype(o_ref.dtype)
        return carry

    lax.fori_loop(0, n_all, finish, 0)


def _lru_mix(xa, ga, conv_w, conv_b, wg, bg, lam, e, n_ctx_rows):
    r, w = xa.shape
    hd = LRU_HEAD_DIM
    ch = SCAN_STEPS * SUBLANE
    assert (r // ch) % 2 == 0, "chunks are processed in pairs"
    kern = functools.partial(_lru_kernel, n_ctx=n_ctx_rows // ch, n_all=r // ch, ch=ch)
    return pl.pallas_call(
        kern,
        grid=(w // hd,),
        in_specs=[
            pl.BlockSpec((r, hd), lambda h: (0, h)),
            pl.BlockSpec((r, hd), lambda h: (0, h)),
            pl.BlockSpec((None, conv_w.shape[1], hd), lambda h: (e, 0, h)),
            pl.BlockSpec((None, 1, hd), lambda h: (e, 0, h)),
            pl.BlockSpec((None, 2, None, hd, 2 * hd), lambda h: (e, 0, h, 0, 0)),
            pl.BlockSpec((None, 2, 1, 2 * hd), lambda h: (e, 0, 0, h)),
            pl.BlockSpec((None, 2, 1, hd), lambda h: (e, 0, 0, h)),
        ],
        out_specs=pl.BlockSpec((r, hd), lambda h: (0, h)),
        out_shape=jax.ShapeDtypeStruct((r, w), BF16),
        scratch_shapes=[pltpu.VMEM((r, hd), F32)] * 2 + [pltpu.VMEM((ch, hd), F32)] * 8,
        compiler_params=_cparams(56, "parallel"),
        name="lru_mix",
    )(xa, ga, conv_w, conv_b, wg, bg, lam)


def _s5_disc_kernel(lre_ref, lim_ref, ldt_ref, bre_ref, bim_ref, are_ref, aim_ref, ore_ref, oim_ref):
    lam_re = lre_ref[...]
    lam_im = lim_ref[...]
    dt = jnp.exp(ldt_ref[...])
    mag = jnp.exp(lam_re * dt)
    ab_re = mag * jnp.cos(lam_im * dt)
    ab_im = mag * jnp.sin(lam_im * dt)
    den = lam_re * lam_re + lam_im * lam_im
    nr = ab_re - 1.0
    f_re = (nr * lam_re + ab_im * lam_im) / den
    f_im = (ab_im * lam_re - nr * lam_im) / den
    b_re = bre_ref[...]
    b_im = bim_ref[...]
    are_ref[...] = ab_re
    aim_ref[...] = ab_im
    ore_ref[...] = f_re * b_re - f_im * b_im
    oim_ref[...] = f_re * b_im + f_im * b_re


def _s5_discretise(lam_re, lam_im, log_dt, b_re, b_im):
    lead = b_re.shape[:-2]
    p, h = b_re.shape[-2:]
    nrow = math.prod(lead)
    flat = lambda v: jnp.broadcast_to(v[..., None], b_re.shape).reshape(nrow, p * h)
    shp = jax.ShapeDtypeStruct((nrow, p * h), F32)
    a_re, a_im, o_re, o_im = pl.pallas_call(
        _s5_disc_kernel, out_shape=[shp] * 4, name="s5_discretise",
    )(flat(lam_re), flat(lam_im), flat(log_dt), b_re.reshape(nrow, p * h), b_im.reshape(nrow, p * h))
    unflat = lambda v: v.reshape(*lead, p, h)
    return unflat(a_re)[..., 0], unflat(a_im)[..., 0], unflat(o_re), unflat(o_im)


def _s5_kernel(u_ref, bm_ref, cm_ref, ar_ref, ai_ref, dsk_ref, o_ref, yf_ref, yb_ref, *v_refs, n_ctx, n_all, ch):
    ns = ar_ref.shape[-1]
    tb = ch // SUBLANE
    bufs = (v_refs[0:2], v_refs[2:4])
    h_refs = (v_refs[4:6], v_refs[6:8])
    y_refs = (yf_ref, yb_ref)
    order = (lambda j: j, lambda j: _chunk_order(j, n_ctx, n_all))
    row0 = lambda j, d: pl.multiple_of(order[d](j) * ch, ch)

    def expand(j, slot):
        for d in (0, 1):
            bufs[d][slot][...] = jnp.dot(u_ref[pl.ds(row0(j, d), ch), :], bm_ref[d], preferred_element_type=F32)

    def scan(slot, hs):
        hs = [list(hs[0:2]), list(hs[2:4])]
        ar = [jnp.broadcast_to(ar_ref[d], (SUBLANE, ns)) for d in (0, 1)]
        ai = [jnp.broadcast_to(ai_ref[d], (SUBLANE, ns)) for d in (0, 1)]
        for k in range(tb // 2):
            for d in (0, 1):
                v_ref = bufs[d][slot]
                pair = (2 * k, 2 * k + 1) if d == 0 else (tb - 1 - 2 * k, tb - 2 - 2 * k)
                new = {}
                for t in pair:
                    rr = t * SUBLANE
                    hr, hi = hs[d]
                    nr = ar[d] * hr - ai[d] * hi + v_ref[rr:rr + SUBLANE, 0:ns]
                    ni = ar[d] * hi + ai[d] * hr + v_ref[rr:rr + SUBLANE, ns:2 * ns]
                    hs[d] = new[t] = [nr, ni]
                lo = min(pair)
                rows = slice(lo * SUBLANE, (lo + 2) * SUBLANE)
                for part in (0, 1):
                    both = jnp.concatenate([new[lo][part], new[lo + 1][part]], axis=0)
                    h_refs[d][slot][rows, part * ns:(part + 1) * ns] = both.astype(BF16)
        return (*hs[0], *hs[1])

    def project(j, slot):
        for d in (0, 1):
            y_refs[d][pl.ds(row0(j, d), ch), :] = jnp.dot(h_refs[d][slot][...], cm_ref[d], preferred_element_type=F32)

    last = n_all - 1
    z = jnp.zeros((SUBLANE, ns), F32)
    expand(0, 0)
    hs = scan(0, (z, z, z, z))
    expand(1, 1)

    def chunk_pair(i, hs):
        j = 2 * i
        project(j, 0)
        hs = scan(1, hs)
        expand(jnp.minimum(j + 2, last), 0)
        project(j + 1, 1)
        hs = scan(0, hs)
        expand(jnp.minimum(j + 3, last), 1)
        return hs

    lax.fori_loop(0, n_all // 2, chunk_pair, hs)

    def finish(c, carry):
        r0 = pl.multiple_of(c * ch, ch)
        u = u_ref[pl.ds(r0, ch), :].astype(F32)
        y = yf_ref[pl.ds(r0, ch), :] + yb_ref[pl.ds(r0, ch), :] + dsk_ref[...] * u
        o_ref[pl.ds(r0, ch), :] = jax.nn.gelu(y).astype(o_ref.dtype)
        return carry

    lax.fori_loop(0, n_all, finish, 0)


def _s5_mix(ub, bm, cm, ar, ai, dsk, e, n_ctx_rows):
    r, w = ub.shape
    ns = ar.shape[-1]
    ch = SCAN_STEPS * SUBLANE
    assert (r // ch) % 2 == 0, "chunks are processed in pairs"
    kern = functools.partial(_s5_kernel, n_ctx=n_ctx_rows // ch, n_all=r // ch, ch=ch)
    return pl.pallas_call(
        kern,
        grid=(w // LANE,),
        in_specs=[
            pl.BlockSpec((r, LANE), lambda s: (0, s)),
            pl.BlockSpec((None, 2, None, LANE, 2 * ns), lambda s: (e, 0, s, 0, 0)),
            pl.BlockSpec((None, 2, None, 2 * ns, LANE), lambda s: (e, 0, s, 0, 0)),
            pl.BlockSpec((None, 2, None, 1, ns), lambda s: (e, 0, s, 0, 0)),
            pl.BlockSpec((None, 2, None, 1, ns), lambda s: (e, 0, s, 0, 0)),
            pl.BlockSpec((None, 1, LANE), lambda s: (e, 0, s)),
        ],
        out_specs=pl.BlockSpec((r, LANE), lambda s: (0, s)),
        out_shape=jax.ShapeDtypeStruct((r, w), BF16),
        scratch_shapes=([pltpu.VMEM((r, LANE), F32)] * 2 + [pltpu.VMEM((ch, 2 * ns), F32)] * 4
                        + [pltpu.VMEM((ch, 2 * ns), BF16)] * 4),
        compiler_params=_cparams(54, "parallel"),
        name="s5_mix",
    )(ub, bm, cm, ar, ai, dsk)


def _outproj_even_kernel(ya_ref, yg_ref, gb_ref, *refs, n_res, ctx_blocks):
    x_refs, (mod_ref, gw_ref, gbias_ref, w1_ref, w2_ref, o_ref) = refs[:n_res], refs[n_res:]
    bsz, ts, d = x_refs[0].shape
    yg = yg_ref[...]
    z = jnp.dot(yg, gw_ref[...], preferred_element_type=F32) + gbias_ref[...]
    yb = yg.astype(F32) * _sigmoid(z) * _silu(gb_ref[...].astype(F32))
    o = jnp.dot(ya_ref[...], w1_ref[...], preferred_element_type=F32)
    o = o + jnp.dot(yb.astype(BF16), w2_ref[...], preferred_element_type=F32)
    o = jnp.swapaxes(o.reshape(ts, bsz, d), 0, 1)
    o_ref[...] = _residual_rows(x_refs, ctx_blocks) + mod_ref[...][..., 2 * d:] * o


def _outproj_even(ya, yg, gb, xs, mod5, layer, seg, glu_w, glu_b, w_out, e, ts, ctx_blocks, steps):
    bsz, _, d = xs[0].shape
    wa, wb = ya.shape[1], yg.shape[1]
    tm = ts * bsz
    rows = lambda n: pl.BlockSpec((tm, n), lambda i: (i, 0))
    return pl.pallas_call(
        functools.partial(_outproj_even_kernel, n_res=len(xs), ctx_blocks=ctx_blocks),
        grid=(steps // ts,),
        in_specs=[rows(wa), rows(wb), rows(wb)] + _residual_specs(xs, ts, ctx_blocks) + [
                  pl.BlockSpec((None, None, bsz, 1, 3 * d), lambda i: (layer, seg(i), 0, 0, 0)),
                  pl.BlockSpec((None, wb, wb), lambda i: (e, 0, 0)),
                  pl.BlockSpec((None, 1, wb), lambda i: (e, 0, 0)),
                  pl.BlockSpec((None, wa, d), lambda i: (e, 0, 0)),
                  pl.BlockSpec((None, wb, d), lambda i: (e, wa // wb, 0))],
        out_specs=pl.BlockSpec((bsz, ts, d), lambda i: (0, i, 0)),
        out_shape=jax.ShapeDtypeStruct((bsz, steps, d), F32),
        compiler_params=_cparams(58, "parallel"),
        name="outproj_even",
    )(ya, yg, gb, *xs, mod5, glu_w, glu_b, w_out, w_out)


def _outproj_odd_kernel(o_ref, x_ref, mod_ref, w_ref, *rest, final, n_ctx=None, blocks_per_sample=None):
    d = x_ref.shape[1]
    y = jnp.dot(o_ref[...], w_ref[...], preferred_element_type=F32)
    if final:
        g_ref, out_ref = rest
        out_ref[...] = _rms(x_ref[...] + mod_ref[...][:, 2 * d:] * y, g_ref[...])
    else:
        modc_ref, out_ref = rest
        m_head, m_lat = _segment_mods(mod_ref, modc_ref, blocks_per_sample)
        out_ref[...] = x_ref[...] + _two_segments(y, n_ctx, lambda v: v * m_head[:, 2 * d:], lambda v: v * m_lat[:, 2 * d:])


def _outproj_odd(att, xs2, mod5, layer, w, o, tm, blocks_per_sample, bsz, n_ctx=None, final_g=None):
    r, d = xs2.shape
    wspec = pl.BlockSpec((None, w.shape[1], d), lambda *_: (o, 0, 0))
    if final_g is None:
        blk = lambda n: pl.BlockSpec((tm, n), lambda i: (i, 0))
        mod_seg = lambda seg: pl.BlockSpec((None, None, None, 1, 3 * d),
                                           lambda i: (layer, seg, i // blocks_per_sample, 0, 0))
        return pl.pallas_call(
            functools.partial(_outproj_odd_kernel, final=False, n_ctx=n_ctx, blocks_per_sample=blocks_per_sample),
            grid=(r // tm,),
            in_specs=[blk(att.shape[1]), blk(d), mod_seg(0), wspec, mod_seg(1)],
            out_specs=blk(d),
            out_shape=jax.ShapeDtypeStruct((r, d), F32),
            compiler_params=_cparams(40, "parallel"),
            name="outproj_odd",
        )(att, xs2, mod5, w, mod5)
    nlat = blocks_per_sample - 1
    blk = lambda n: pl.BlockSpec((tm, n), lambda b, j: (b * blocks_per_sample + 1 + j, 0))
    return pl.pallas_call(
        functools.partial(_outproj_odd_kernel, final=True),
        grid=(bsz, nlat),
        in_specs=[blk(att.shape[1]), blk(d),
                  pl.BlockSpec((None, None, None, 1, 3 * d), lambda b, j: (layer, 0, b, 0, 0)),
                  wspec, pl.BlockSpec((1, d), lambda b, j: (0, 0))],
        out_specs=pl.BlockSpec((None, tm, d), lambda b, j: (b, j, 0)),
        out_shape=jax.ShapeDtypeStruct((bsz, nlat * tm, d), F32),
        compiler_params=_cparams(32, "parallel", "parallel"),
        name="outproj_final",
    )(att, xs2, mod5, w, final_g.reshape(1, d))


def _rope_fold(v, tab):
    w = v * tab
    return w + pltpu.roll(w, LANE // 2, axis=1)


def _segment_mods(mod_lat_ref, mod_ctx_ref, blocks_per_sample):
    first = pl.program_id(0) % blocks_per_sample == 0
    m_lat = mod_lat_ref[...]
    return jnp.where(first, mod_ctx_ref[...], m_lat), m_lat


def _two_segments(y, n_head, f_head, f_rest):
    if n_head == y.shape[0]:
        return f_head(y)
    return jnp.concatenate([f_head(y[:n_head]), f_rest(y[n_head:])], axis=0)


def _mla_in_kernel(x_ref, mod_ref, modc_ref, g_ref, w_ref, tab_ref, qg_ref, kvg_ref, wq_ref, wkv_ref,
                   gate_ref, q_ref, k_ref, v_ref, *, q_rank, kv_rank, n_ctx, blocks_per_sample):
    heads = q_ref.shape[0]
    d = x_ref.shape[1]
    m_head, m_lat = _segment_mods(mod_ref, modc_ref, blocks_per_sample)
    y = _rms(x_ref[...], g_ref[...])
    shift_scale = lambda m: (lambda v: v * (1.0 + m[:, d:2 * d]) + m[:, :d])
    n = _two_segments(y, n_ctx, shift_scale(m_head), shift_scale(m_lat)).astype(BF16)
    proj = lambda lo, hi: jnp.dot(n, w_ref[:, lo:hi], preferred_element_type=F32)
    off_g = q_rank + kv_rank
    off_kr = off_g + gate_ref.shape[1]
    gate_ref[...] = proj(off_g, off_kr).astype(gate_ref.dtype)
    tab = tab_ref[...]
    q = jnp.dot(_rms(proj(0, q_rank), qg_ref[...]).astype(BF16), wq_ref[...], preferred_element_type=F32)
    kv = jnp.dot(_rms(proj(q_rank, off_g), kvg_ref[...]).astype(BF16), wkv_ref[...], preferred_element_type=F32)
    kr = _rope_fold(proj(off_kr, off_kr + 2 * MLA_ROPE), tab)[:, :MLA_ROPE].astype(BF16)
    per = MLA_NOPE + 2 * MLA_ROPE
    for h in range(heads):
        qh = q[:, h * per:(h + 1) * per]
        q_ref[h, :, 0:MLA_NOPE] = (qh[:, :MLA_NOPE] * Q_PRESCALE).astype(BF16)
        qr = _rope_fold(qh[:, MLA_NOPE:], tab)[:, :MLA_ROPE] * Q_PRESCALE
        q_ref[h, :, MLA_NOPE:MLA_NOPE + MLA_ROPE] = qr.astype(BF16)
        kvh = kv[:, h * (MLA_NOPE + MLA_V):(h + 1) * (MLA_NOPE + MLA_V)]
        k_ref[h, :, 0:MLA_NOPE] = kvh[:, :MLA_NOPE].astype(BF16)
        k_ref[h, :, MLA_NOPE:MLA_NOPE + MLA_ROPE] = kr
        v_ref[h] = kvh[:, MLA_NOPE:].astype(BF16)


def _mla_in(xs2, mod5, norm_g3, w_in, tab, q_norm, kv_norm, wq, wkv, layer, o, tm, blocks_per_sample, n_ctx):
    r, d = xs2.shape
    heads = MLA_HEADS
    dk = MLA_NOPE + MLA_ROPE
    q_rank, kv_rank = q_norm.shape[-1], kv_norm.shape[-1]
    rows = lambda n: pl.BlockSpec((tm, n), lambda i: (i, 0))
    sel = lambda a, idx: pl.BlockSpec((None,) + a.shape[1:], lambda i: (idx, 0, 0))
    hrows = lambda n: pl.BlockSpec((heads, tm, n), lambda i: (0, i, 0))
    mod_seg = lambda seg: pl.BlockSpec((None, None, None, 1, 3 * d),
                                       lambda i: (layer, seg, i // blocks_per_sample, 0, 0))
    return pl.pallas_call(
        functools.partial(_mla_in_kernel, q_rank=q_rank, kv_rank=kv_rank, n_ctx=n_ctx,
                          blocks_per_sample=blocks_per_sample),
        grid=(r // tm,),
        in_specs=[rows(d), mod_seg(0), mod_seg(1), sel(norm_g3, layer), sel(w_in, o),
                  pl.BlockSpec((tm, LANE), lambda i: (i % blocks_per_sample, 0)),
                  sel(q_norm, o), sel(kv_norm, o), sel(wq, o), sel(wkv, o)],
        out_specs=[rows(heads * MLA_V), hrows(dk), hrows(dk), hrows(MLA_V)],
        out_shape=[jax.ShapeDtypeStruct((r, heads * MLA_V), BF16),
                   jax.ShapeDtypeStruct((heads, r, dk), BF16), jax.ShapeDtypeStruct((heads, r, dk), BF16),
                   jax.ShapeDtypeStruct((heads, r, MLA_V), BF16)],
        compiler_params=_cparams(56, "parallel"),
        name="mla_in",
    )(xs2, mod5, mod5, norm_g3, w_in, tab, q_norm, kv_norm, wq, wkv)


def _attn_kernel(q_ref, k_ref, v_ref, g_ref, o_ref, vx_ref, s0_ref, s1_ref, *, with_ctx, n_ctx, tq):
    per = k_ref.shape[0]
    n_lat = (per - n_ctx) // tq
    vx_ref[:, 0:MLA_V] = v_ref[...]
    vx_ref[:, MLA_V:] = jnp.ones((per, MLA_V), BF16)

    def scores(r0, nq, s_ref, nk):
        s_ref[0:nq, 0:nk] = lax.dot_general(q_ref[r0:r0 + nq, :], k_ref[0:nk, :], (((1,), (1,)), ((), ())),
                                            preferred_element_type=F32)

    def finish(r0, nq, s_ref, nk):
        s = s_ref[0:nq, 0:nk]
        p = jnp.exp2(s - jnp.max(s, axis=-1, keepdims=True))
        ox = jnp.dot(p.astype(BF16), vx_ref[0:nk, :], preferred_element_type=F32)
        o = ox[:, :MLA_V] / ox[:, MLA_V:]
        g = g_ref[r0:r0 + nq, :].astype(F32)
        o_ref[r0:r0 + nq, :] = (o * _silu(g)).astype(o_ref.dtype)

    if with_ctx:
        scores(0, n_ctx, s0_ref, n_ctx)
        finish(0, n_ctx, s0_ref, n_ctx)
    else:
        o_ref[0:n_ctx, :] = jnp.zeros((n_ctx, MLA_V), o_ref.dtype)

    s_refs = (s0_ref, s1_ref)
    scores(n_ctx, tq, s0_ref, per)
    for j in range(n_lat):
        if j + 1 < n_lat:
            scores(n_ctx + (j + 1) * tq, tq, s_refs[(j + 1) % 2], per)
        finish(n_ctx + j * tq, tq, s_refs[j % 2], per)


def _attention(q, k, v, gate, bsz, n_ctx, with_ctx):
    heads, r, dk = q.shape
    per = r // bsz
    tq = min(ATTN_Q_ROWS, per - n_ctx)
    assert (per - n_ctx) % tq == 0 and n_ctx <= tq
    blk = lambda n: pl.BlockSpec((None, per, n), lambda b, h: (h, b, 0))
    return pl.pallas_call(
        functools.partial(_attn_kernel, with_ctx=with_ctx, n_ctx=n_ctx, tq=tq),
        grid=(bsz, heads),
        in_specs=[blk(dk), blk(dk), blk(MLA_V), pl.BlockSpec((per, MLA_V), lambda b, h: (b, h))],
        out_specs=pl.BlockSpec((per, MLA_V), lambda b, h: (b, h)),
        out_shape=jax.ShapeDtypeStruct((r, heads * MLA_V), BF16),
        scratch_shapes=[pltpu.VMEM((per, 2 * MLA_V), BF16), pltpu.VMEM((tq, per), F32), pltpu.VMEM((tq, per), F32)],
        compiler_params=_cparams(40, "parallel", "parallel"),
        name="mla_attention",
    )(q, k, v, gate)


def _s5_matrices(a_re, a_im, bb_re, bb_im, c_re, c_im):
    gps = LANE // S5_GROUP
    ne, nd, g, p, h = bb_re.shape
    slabs = g // gps
    eye = jnp.eye(gps, dtype=F32)
    bd_in = lambda m: (jnp.swapaxes(m.reshape(ne, nd, slabs, gps, p, h), -1, -2)[..., None, :]
                       * eye[:, None, :, None]).reshape(ne, nd, slabs, gps * h, gps * p)
    bd_out = lambda m: (jnp.transpose(m.reshape(ne, nd, slabs, gps, h, p), (0, 1, 2, 5, 3, 4))[:, :, :, None]
                        * eye[:, None, :, None]).reshape(ne, nd, slabs, gps * p, gps * h)
    bm = jnp.concatenate([bd_in(bb_re), bd_in(bb_im)], axis=-1).astype(BF16)
    cm = jnp.concatenate([bd_out(c_re), -bd_out(c_im)], axis=-2).astype(BF16)
    ar = a_re.reshape(ne, nd, slabs, 1, gps * p)
    ai = a_im.reshape(ne, nd, slabs, 1, gps * p)
    return bm, cm, ar, ai


def _rope_table(n_ctx, n_lat):
    axis = MLA_ROPE // 2
    rows = n_lat // GRID_W
    row = jnp.repeat(jnp.arange(rows, dtype=F32), GRID_W)
    col = jnp.tile(jnp.arange(GRID_W, dtype=F32), rows)
    inv = ROPE_BASE ** (-jnp.arange(0, axis, 2, dtype=F32) / axis)
    ang = jnp.concatenate([row[:, None] * inv, col[:, None] * inv], axis=-1)
    cos, sin = jnp.cos(ang), jnp.sin(ang)
    lat = jnp.concatenate([cos, cos, -sin, sin], axis=-1)
    one, zero = jnp.ones((n_ctx, axis), F32), jnp.zeros((n_ctx, axis), F32)
    return jnp.concatenate([jnp.concatenate([one, one, zero, zero], axis=-1), lat], axis=0)


def _swap_halves(w):
    half = w.shape[-1] // 2
    return jnp.concatenate([w[..., half:], w[..., :half]], axis=-1)


def kernel(x, c, ctx, c_ctx, norm_g, mod_w, mod_b, ev_w_in, lru_conv_w, lru_conv_b, lru_wr, lru_br, lru_wi, lru_bi, lru_lam, s5_lam_re, s5_lam_im, s5_log_dt, s5_b_re, s5_b_im, s5_c_re, s5_c_im, s5_d, s5_glu_w, s5_glu_b, ev_w_out, mla_w_in, mla_q_norm, mla_w_uq, mla_kv_norm, mla_w_ukv, mla_w_out, final_g):
    bsz, n_lat, d = x.shape
    n_ctx = ctx.shape[1]
    steps = n_ctx + n_lat
    depth = mod_w.shape[0]
    n_even, n_odd = ev_w_in.shape[0], mla_w_in.shape[0]
    assert bsz == SUBLANE, "time-major layout maps the batch onto the sublanes of one vreg"
    assert depth % 2 == 0, "the last layer is an attention layer"
    lru_w = lru_conv_w.shape[-1]
    s5_w = s5_d.shape[-2] * s5_d.shape[-1]
    nh = lru_w // LRU_HEAD_DIM
    heads = MLA_HEADS
    q_rank, kv_rank = mla_q_norm.shape[-1], mla_kv_norm.shape[-1]

    cond = jnp.concatenate([c, jnp.broadcast_to(c_ctx[None], (bsz, d))], axis=0)
    mod5 = _modulation(cond, mod_w, mod_b).reshape(depth, 2, bsz, 1, 3 * d)
    norm_g3 = norm_g.reshape(depth, 1, d)

    ev_w_in_b = ev_w_in.astype(BF16)
    ev_w_out_b = ev_w_out.astype(BF16)
    glu_w_b = s5_glu_w.astype(BF16)
    glu_b3 = s5_glu_b.reshape(n_even, 1, s5_w)
    conv_b3 = lru_conv_b.reshape(n_even, 1, lru_w)
    wg = (0.5 * jnp.concatenate([lru_wr, lru_wi], axis=-1)).astype(BF16)
    bg = 0.5 * jnp.concatenate([lru_br.reshape(n_even, 2, nh, LRU_HEAD_DIM), lru_bi.reshape(n_even, 2, nh, LRU_HEAD_DIM)],
                               axis=-1).reshape(n_even, 2, 1, 2 * lru_w)
    lam4 = lru_lam.reshape(n_even, 2, 1, lru_w)
    a_re, a_im, bb_re, bb_im = _s5_discretise(s5_lam_re, s5_lam_im, s5_log_dt, s5_b_re, s5_b_im)
    s5_bm, s5_cm, s5_ar, s5_ai = _s5_matrices(a_re, a_im, bb_re, bb_im, s5_c_re, s5_c_im)
    dsk3 = s5_d.reshape(n_even, 1, s5_w)

    off_kr, off_g = q_rank + kv_rank, q_rank + kv_rank + MLA_ROPE
    w_kr = mla_w_in[..., off_kr:off_g]
    mla_w_in_b = jnp.concatenate([mla_w_in[..., :off_kr], mla_w_in[..., off_g:], w_kr, _swap_halves(w_kr)],
                                 axis=-1).astype(BF16)
    wq = mla_w_uq.reshape(n_odd, q_rank, heads, MLA_NOPE + MLA_ROPE)
    wq_b = jnp.concatenate([wq, _swap_halves(wq[..., MLA_NOPE:])], axis=-1).reshape(n_odd, q_rank, -1).astype(BF16)
    wkv_b = mla_w_ukv.astype(BF16)
    mla_w_out_b = mla_w_out.astype(BF16)
    q_norm3 = mla_q_norm.reshape(n_odd, 1, q_rank)
    kv_norm3 = mla_kv_norm.reshape(n_odd, 1, kv_rank)
    rope_tab = _rope_table(n_ctx, n_lat)

    ts = PROJ_STEPS
    ctx_blocks = n_ctx // ts
    seg_tm = lambda i: (i < ctx_blocks).astype(jnp.int32)
    tq = n_ctx
    blocks_per_sample = steps // tq
    rows = bsz * steps
    tm_odd = ODD_ROWS if steps % ODD_ROWS == 0 and n_ctx < ODD_ROWS else tq
    assert steps % tm_odd == 0 and n_ctx <= tm_odd

    res = (ctx, x)
    for l in range(depth):
        if l % 2 == 0:
            e = l // 2
            xa, ga, ub, gb = _inproj(res, mod5, norm_g3, ev_w_in_b, l, e, seg_tm,
                                     (lru_w, lru_w, s5_w, s5_w), (BF16,) * 4, ts, ctx_blocks, steps)
            ya = _lru_mix(xa, ga, lru_conv_w, conv_b3, wg, bg, lam4, e, n_ctx * bsz)
            yg = _s5_mix(ub, s5_bm, s5_cm, s5_ar, s5_ai, dsk3, e, n_ctx * bsz)
            xs = _outproj_even(ya, yg, gb, res, mod5, l, seg_tm, glu_w_b, glu_b3, ev_w_out_b, e, ts, ctx_blocks, steps)
            res = (xs,)
        else:
            o = l // 2
            with_ctx = l < depth - 1
            xs2 = xs.reshape(rows, d)
            gate, q, k, v = _mla_in(xs2, mod5, norm_g3, mla_w_in_b, rope_tab, q_norm3, kv_norm3, wq_b, wkv_b,
                                    l, o, tm_odd, steps // tm_odd, n_ctx)
            att = _attention(q, k, v, gate, bsz, n_ctx, with_ctx)
            if with_ctx:
                xs = _outproj_odd(att, xs2, mod5, l, mla_w_out_b, o, tm_odd, steps // tm_odd, bsz,
                                  n_ctx=n_ctx).reshape(bsz, steps, d)
                res = (xs,)
            else:
                xs = _outproj_odd(att, xs2, mod5, l, mla_w_out_b, o, tq, blocks_per_sample, bsz, final_g=final_g)
    return xs
```

```python
import functools
import math

import jax
import jax.numpy as jnp
from jax import lax
from jax.experimental import pallas as pl
from jax.experimental.pallas import tpu as pltpu

F32 = jnp.float32
BF16 = jnp.bfloat16

NORM_EPS = 1e-6
LRU_C = 8.0
LRU_HEAD_DIM = 128
S5_GROUP = 16
MLA_HEADS = 8
MLA_NOPE = 128
MLA_ROPE = 64
MLA_V = 128
MLA_SCALE = 1.0 / math.sqrt(MLA_NOPE + MLA_ROPE)
LOG2E = math.log2(math.e)
Q_PRESCALE = MLA_SCALE * LOG2E
ROPE_BASE = 10000.0
GRID_W = 64

LANE = 128
SUBLANE = 8
SCAN_STEPS = 32
PROJ_STEPS = 128
ODD_ROWS = 768
ATTN_Q_ROWS = 1024
MIB = 1024 * 1024


def _cparams(vmem_mib, *sem):
    return pltpu.CompilerParams(dimension_semantics=tuple(sem), vmem_limit_bytes=int(vmem_mib * MIB))


def _sigmoid(x):
    return 0.5 * jnp.tanh(0.5 * x) + 0.5


def _silu(x):
    return x * _sigmoid(x)


def _sqrt_nonneg(x):
    return jnp.where(x > 0.0, x * lax.rsqrt(x), 0.0)


def _rms(x, g):
    return x * lax.rsqrt(jnp.mean(x * x, axis=-1, keepdims=True) + NORM_EPS) * g


def _mod_kernel(c_ref, w_ref, b_ref, o_ref):
    a = _silu(c_ref[...])
    o_ref[...] = jnp.dot(a, w_ref[...], preferred_element_type=F32, precision=lax.Precision.HIGHEST) + b_ref[...]


def _modulation(cond, mod_w, mod_b):
    depth, d, n = mod_w.shape
    rows = cond.shape[0]
    tn = min(n, 1024)
    return pl.pallas_call(
        _mod_kernel,
        grid=(depth, n // tn),
        in_specs=[
            pl.BlockSpec((rows, d), lambda l, j: (0, 0)),
            pl.BlockSpec((None, d, tn), lambda l, j: (l, 0, j)),
            pl.BlockSpec((None, 1, tn), lambda l, j: (l, 0, j)),
        ],
        out_specs=pl.BlockSpec((None, rows, tn), lambda l, j: (l, 0, j)),
        out_shape=jax.ShapeDtypeStruct((depth, rows, n), F32),
        compiler_params=_cparams(32, "parallel", "parallel"),
        name="modulation",
    )(cond, mod_w, mod_b.reshape(depth, 1, n))


def _residual_specs(xs, ts, ctx_blocks):
    bsz, _, d = xs[0].shape
    if len(xs) == 1:
        return [pl.BlockSpec((bsz, ts, d), lambda i: (0, i, 0))]
    return [pl.BlockSpec((bsz, ts, d), lambda i: (0, jnp.minimum(i, ctx_blocks - 1), 0)),
            pl.BlockSpec((bsz, ts, d), lambda i: (0, jnp.maximum(i - ctx_blocks, 0), 0))]


def _residual_rows(x_refs, ctx_blocks):
    if len(x_refs) == 1:
        return x_refs[0][...]
    return jnp.where(pl.program_id(0) < ctx_blocks, x_refs[0][...], x_refs[1][...])


def _inproj_kernel(*refs, splits, n_res, ctx_blocks):
    x_refs, (mod_ref, g_ref, w_ref), out_refs = refs[:n_res], refs[n_res:n_res + 3], refs[n_res + 3:]
    x = _residual_rows(x_refs, ctx_blocks)
    d = x.shape[-1]
    m = mod_ref[...]
    n = _rms(x, g_ref[...]) * (1.0 + m[..., d:2 * d]) + m[..., :d]
    n = jnp.swapaxes(n, 0, 1).reshape(-1, d).astype(BF16)
    off = 0
    for o_ref, width in zip(out_refs, splits):
        o_ref[...] = jnp.dot(n, w_ref[:, off:off + width], preferred_element_type=F32).astype(o_ref.dtype)
        off += width


def _inproj(xs, mod5, g3, w3, layer, e, seg, splits, dtypes, ts, ctx_blocks, steps):
    bsz, _, d = xs[0].shape
    n = w3.shape[2]
    tm = ts * bsz
    return pl.pallas_call(
        functools.partial(_inproj_kernel, splits=splits, n_res=len(xs), ctx_blocks=ctx_blocks),
        grid=(steps // ts,),
        in_specs=_residual_specs(xs, ts, ctx_blocks) + [
            pl.BlockSpec((None, None, bsz, 1, 3 * d), lambda i: (layer, seg(i), 0, 0, 0)),
            pl.BlockSpec((None, 1, d), lambda i: (layer, 0, 0)),
            pl.BlockSpec((None, d, n), lambda i: (e, 0, 0))],
        out_specs=[pl.BlockSpec((tm, s), lambda i: (i, 0)) for s in splits],
        out_shape=[jax.ShapeDtypeStruct((steps * bsz, s), dt) for s, dt in zip(splits, dtypes)],
        compiler_params=_cparams(58, "parallel"),
        name="inproj",
    )(*xs, mod5, g3, w3)


def _softplus(x):
    return jnp.log1p(jnp.exp(-jnp.abs(x))) + jnp.maximum(x, 0.0)


def _chunk_order(j, n_ctx, n_all):
    return jnp.where(j < n_ctx, n_ctx - 1 - j, n_all - 1 - (j - n_ctx))


def _lru_kernel(xa_ref, ga_ref, cw_ref, cb_ref, wg_ref, bg_ref, lam_ref, o_ref,
                hf_ref, hb_ref, *ab_refs, n_ctx, n_all, ch):
    rows_total = xa_ref.shape[0]
    hd = xa_ref.shape[1]
    tb = ch // SUBLANE
    halo = 2 * SUBLANE
    row = lax.broadcasted_iota(jnp.int32, (ch, hd), 0)
    cw = cw_ref[...]
    cb = cb_ref[...]
    bufs = ((ab_refs[0:2], ab_refs[2:4]), (ab_refs[4:6], ab_refs[6:8]))
    dst = (hf_ref, hb_ref)
    order = (lambda j: j, lambda j: _chunk_order(j, n_ctx, n_all))
    k_log2a = [(-0.5 * LRU_C * LOG2E) * _softplus(-lam_ref[d]) for d in (0, 1)]

    def conv(c):
        r0 = pl.multiple_of(c * ch, ch)
        first = jnp.logical_or(c == 0, c == n_ctx)
        last = jnp.logical_or(c == n_ctx - 1, c == n_all - 1)
        rp = pl.multiple_of(jnp.maximum(r0 - halo, 0), halo)
        rn = pl.multiple_of(jnp.minimum(r0 + ch, rows_total - halo), halo)
        ext = jnp.concatenate([xa_ref[pl.ds(rp, halo), :], xa_ref[pl.ds(r0, ch), :],
                               xa_ref[pl.ds(rn, halo), :]], axis=0).astype(F32)
        xm2 = jnp.where(jnp.logical_and(first, row < 2 * SUBLANE), 0.0, ext[0:ch])
        xm1 = jnp.where(jnp.logical_and(first, row < SUBLANE), 0.0, ext[SUBLANE:SUBLANE + ch])
        x0 = ext[2 * SUBLANE:2 * SUBLANE + ch]
        xp1 = jnp.where(jnp.logical_and(last, row >= ch - SUBLANE), 0.0, ext[3 * SUBLANE:3 * SUBLANE + ch])
        return cb + cw[0:1] * xm2 + cw[1:2] * xm1 + cw[2:3] * x0 + cw[3:4] * xp1

    def coeffs(c, d, a_ref, b_ref):
        u = conv(c)
        t = jnp.tanh(jnp.dot(u.astype(BF16), wg_ref[d], preferred_element_type=F32) + bg_ref[d])
        a = jnp.exp2(t[:, :hd] * k_log2a[d] + k_log2a[d])
        a_ref[...] = a
        hu = 0.5 * u
        b_ref[...] = _sqrt_nonneg(1.0 - a * a) * (t[:, hd:] * hu + hu)

    def scan(c, d, a_ref, b_ref, h):
        r0 = pl.multiple_of(c * ch, ch)
        for t in range(tb):
            rr = (t if d == 0 else tb - 1 - t) * SUBLANE
            h = a_ref[rr:rr + SUBLANE, :] * h + b_ref[rr:rr + SUBLANE, :]
            dst[d][pl.ds(r0 + rr, SUBLANE), :] = h
        return h

    for d in (0, 1):
        coeffs(order[d](0), d, *bufs[d][0])

    def chunk_pair(i, hs):
        hs = list(hs)
        for slot in (0, 1):
            j = 2 * i + slot
            nxt = jnp.minimum(j + 1, n_all - 1)
            for d in (0, 1):
                coeffs(order[d](nxt), d, *bufs[d][1 - slot])
            for d in (0, 1):
                hs[d] = scan(order[d](j), d, *bufs[d][slot], hs[d])
        return tuple(hs)

    h0 = jnp.zeros((SUBLANE, hd), F32)
    lax.fori_loop(0, n_all // 2, chunk_pair, (h0, h0))

    def finish(c, carry):
        r0 = pl.multiple_of(c * ch, ch)
        g = ga_ref[pl.ds(r0, ch), :].astype(F32)
        o_ref[pl.ds(r0, ch), :] = ((hf_ref[pl.ds(r0, ch), :] + hb_ref[pl.ds(r0, ch), :]) * _silu(g)).astype(o_ref.dtype)
        return carry

    lax.fori_loop(0, n_all, finish, 0)


def _lru_mix(xa, ga, conv_w, conv_b, wg, bg, lam, e, n_ctx_rows):
    r, w = xa.shape
    hd = LRU_HEAD_DIM
    ch = SCAN_STEPS * SUBLANE
    assert (r // ch) % 2 == 0, "chunks are processed in pairs"
    kern = functools.partial(_lru_kernel, n_ctx=n_ctx_rows // ch, n_all=r // ch, ch=ch)
    return pl.pallas_call(
        kern,
        grid=(w // hd,),
        in_specs=[
            pl.BlockSpec((r, hd), lambda h: (0, h)),
            pl.BlockSpec((r, hd), lambda h: (0, h)),
            pl.BlockSpec((None, conv_w.shape[1], hd), lambda h: (e, 0, h)),
            pl.BlockSpec((None, 1, hd), lambda h: (e, 0, h)),
            pl.BlockSpec((None, 2, None, hd, 2 * hd), lambda h: (e, 0, h, 0, 0)),
            pl.BlockSpec((None, 2, 1, 2 * hd), lambda h: (e, 0, 0, h)),
            pl.BlockSpec((None, 2, 1, hd), lambda h: (e, 0, 0, h)),
        ],
        out_specs=pl.BlockSpec((r, hd), lambda h: (0, h)),
        out_shape=jax.ShapeDtypeStruct((r, w), BF16),
        scratch_shapes=[pltpu.VMEM((r, hd), F32)] * 2 + [pltpu.VMEM((ch, hd), F32)] * 8,
        compiler_params=_cparams(56, "parallel"),
        name="lru_mix",
    )(xa, ga, conv_w, conv_b, wg, bg, lam)


def _s5_disc_kernel(lre_ref, lim_ref, ldt_ref, bre_ref, bim_ref, are_ref, aim_ref, ore_ref, oim_ref):
    lam_re = lre_ref[...]
    lam_im = lim_ref[...]
    dt = jnp.exp(ldt_ref[...])
    mag = jnp.exp(lam_re * dt)
    ab_re = mag * jnp.cos(lam_im * dt)
    ab_im = mag * jnp.sin(lam_im * dt)
    den = lam_re * lam_re + lam_im * lam_im
    nr = ab_re - 1.0
    f_re = (nr * lam_re + ab_im * lam_im) / den
    f_im = (ab_im * lam_re - nr * lam_im) / den
    b_re = bre_ref[...]
    b_im = bim_ref[...]
    are_ref[...] = ab_re
    aim_ref[...] = ab_im
    ore_ref[...] = f_re * b_re - f_im * b_im
    oim_ref[...] = f_re * b_im + f_im * b_re


def _s5_discretise(lam_re, lam_im, log_dt, b_re, b_im):
    lead = b_re.shape[:-2]
    p, h = b_re.shape[-2:]
    nrow = math.prod(lead)
    flat = lambda v: jnp.broadcast_to(v[..., None], b_re.shape).reshape(nrow, p * h)
    shp = jax.ShapeDtypeStruct((nrow, p * h), F32)
    a_re, a_im, o_re, o_im = pl.pallas_call(
        _s5_disc_kernel, out_shape=[shp] * 4, name="s5_discretise",
    )(flat(lam_re), flat(lam_im), flat(log_dt), b_re.reshape(nrow, p * h), b_im.reshape(nrow, p * h))
    unflat = lambda v: v.reshape(*lead, p, h)
    return unflat(a_re)[..., 0], unflat(a_im)[..., 0], unflat(o_re), unflat(o_im)


def _s5_kernel(u_ref, bm_ref, cm_ref, ar_ref, ai_ref, dsk_ref, o_ref, yf_ref, yb_ref, *v_refs, n_ctx, n_all, ch):
    ns = ar_ref.shape[-1]
    tb = ch // SUBLANE
    bufs = (v_refs[0:2], v_refs[2:4])
    h_refs = (v_refs[4:6], v_refs[6:8])
    y_refs = (yf_ref, yb_ref)
    order = (lambda j: j, lambda j: _chunk_order(j, n_ctx, n_all))
    row0 = lambda j, d: pl.multiple_of(order[d](j) * ch, ch)

    def expand(j, slot):
        for d in (0, 1):
            bufs[d][slot][...] = jnp.dot(u_ref[pl.ds(row0(j, d), ch), :], bm_ref[d], preferred_element_type=F32)

    def scan(slot, hs):
        hs = [list(hs[0:2]), list(hs[2:4])]
        ar = [jnp.broadcast_to(ar_ref[d], (SUBLANE, ns)) for d in (0, 1)]
        ai = [jnp.broadcast_to(ai_ref[d], (SUBLANE, ns)) for d in (0, 1)]
        for k in range(tb // 2):
            for d in (0, 1):
                v_ref = bufs[d][slot]
                pair = (2 * k, 2 * k + 1) if d == 0 else (tb - 1 - 2 * k, tb - 2 - 2 * k)
                new = {}
                for t in pair:
                    rr = t * SUBLANE
                    hr, hi = hs[d]
                    nr = ar[d] * hr - ai[d] * hi + v_ref[rr:rr + SUBLANE, 0:ns]
                    ni = ar[d] * hi + ai[d] * hr + v_ref[rr:rr + SUBLANE, ns:2 * ns]
                    hs[d] = new[t] = [nr, ni]
                lo = min(pair)
                rows = slice(lo * SUBLANE, (lo + 2) * SUBLANE)
                for part in (0, 1):
                    both = jnp.concatenate([new[lo][part], new[lo + 1][part]], axis=0)
                    h_refs[d][slot][rows, part * ns:(part + 1) * ns] = both.astype(BF16)
        return (*hs[0], *hs[1])

    def project(j, slot):
        for d in (0, 1):
            y_refs[d][pl.ds(row0(j, d), ch), :] = jnp.dot(h_refs[d][slot][...], cm_ref[d], preferred_element_type=F32)

    last = n_all - 1
    z = jnp.zeros((SUBLANE, ns), F32)
    expand(0, 0)
    hs = scan(0, (z, z, z, z))
    expand(1, 1)

    def chunk_pair(i, hs):
        j = 2 * i
        project(j, 0)
        hs = scan(1, hs)
        expand(jnp.minimum(j + 2, last), 0)
        project(j + 1, 1)
        hs = scan(0, hs)
        expand(jnp.minimum(j + 3, last), 1)
        return hs

    lax.fori_loop(0, n_all // 2, chunk_pair, hs)

    def finish(c, carry):
        r0 = pl.multiple_of(c * ch, ch)
        u = u_ref[pl.ds(r0, ch), :].astype(F32)
        y = yf_ref[pl.ds(r0, ch), :] + yb_ref[pl.ds(r0, ch), :] + dsk_ref[...] * u
        o_ref[pl.ds(r0, ch), :] = jax.nn.gelu(y).astype(o_ref.dtype)
        return carry

    lax.fori_loop(0, n_all, finish, 0)


def _s5_mix(ub, bm, cm, ar, ai, dsk, e, n_ctx_rows):
    r, w = ub.shape
    ns = ar.shape[-1]
    ch = SCAN_STEPS * SUBLANE
    assert (r // ch) % 2 == 0, "chunks are processed in pairs"
    kern = functools.partial(_s5_kernel, n_ctx=n_ctx_rows // ch, n_all=r // ch, ch=ch)
    return pl.pallas_call(
        kern,
        grid=(w // LANE,),
        in_specs=[
            pl.BlockSpec((r, LANE), lambda s: (0, s)),
            pl.BlockSpec((None, 2, None, LANE, 2 * ns), lambda s: (e, 0, s, 0, 0)),
            pl.BlockSpec((None, 2, None, 2 * ns, LANE), lambda s: (e, 0, s, 0, 0)),
            pl.BlockSpec((None, 2, None, 1, ns), lambda s: (e, 0, s, 0, 0)),
            pl.BlockSpec((None, 2, None, 1, ns), lambda s: (e, 0, s, 0, 0)),
            pl.BlockSpec((None, 1, LANE), lambda s: (e, 0, s)),
        ],
        out_specs=pl.BlockSpec((r, LANE), lambda s: (0, s)),
        out_shape=jax.ShapeDtypeStruct((r, w), BF16),
        scratch_shapes=([pltpu.VMEM((r, LANE), F32)] * 2 + [pltpu.VMEM((ch, 2 * ns), F32)] * 4
                        + [pltpu.VMEM((ch, 2 * ns), BF16)] * 4),
        compiler_params=_cparams(54, "parallel"),
        name="s5_mix",
    )(ub, bm, cm, ar, ai, dsk)


def _outproj_even_kernel(ya_ref, yg_ref, gb_ref, *refs, n_res, ctx_blocks):
    x_refs, (mod_ref, gw_ref, gbias_ref, w1_ref, w2_ref, o_ref) = refs[:n_res], refs[n_res:]
    bsz, ts, d = x_refs[0].shape
    yg = yg_ref[...]
    z = jnp.dot(yg, gw_ref[...], preferred_element_type=F32) + gbias_ref[...]
    yb = yg.astype(F32) * _sigmoid(z) * _silu(gb_ref[...].astype(F32))
    o = jnp.dot(ya_ref[...], w1_ref[...], preferred_element_type=F32)
    o = o + jnp.dot(yb.astype(BF16), w2_ref[...], preferred_element_type=F32)
    o = jnp.swapaxes(o.reshape(ts, bsz, d), 0, 1)
    o_ref[...] = _residual_rows(x_refs, ctx_blocks) + mod_ref[...][..., 2 * d:] * o


def _outproj_even(ya, yg, gb, xs, mod5, layer, seg, glu_w, glu_b, w_out, e, ts, ctx_blocks, steps):
    bsz, _, d = xs[0].shape
    wa, wb = ya.shape[1], yg.shape[1]
    tm = ts * bsz
    rows = lambda n: pl.BlockSpec((tm, n), lambda i: (i, 0))
    return pl.pallas_call(
        functools.partial(_outproj_even_kernel, n_res=len(xs), ctx_blocks=ctx_blocks),
        grid=(steps // ts,),
        in_specs=[rows(wa), rows(wb), rows(wb)] + _residual_specs(xs, ts, ctx_blocks) + [
                  pl.BlockSpec((None, None, bsz, 1, 3 * d), lambda i: (layer, seg(i), 0, 0, 0)),
                  pl.BlockSpec((None, wb, wb), lambda i: (e, 0, 0)),
                  pl.BlockSpec((None, 1, wb), lambda i: (e, 0, 0)),
                  pl.BlockSpec((None, wa, d), lambda i: (e, 0, 0)),
                  pl.BlockSpec((None, wb, d), lambda i: (e, wa // wb, 0))],
        out_specs=pl.BlockSpec((bsz, ts, d), lambda i: (0, i, 0)),
        out_shape=jax.ShapeDtypeStruct((bsz, steps, d), F32),
        compiler_params=_cparams(58, "parallel"),
        name="outproj_even",
    )(ya, yg, gb, *xs, mod5, glu_w, glu_b, w_out, w_out)


def _outproj_odd_kernel(o_ref, x_ref, mod_ref, w_ref, *rest, final, n_ctx=None, blocks_per_sample=None):
    d = x_ref.shape[1]
    y = jnp.dot(o_ref[...], w_ref[...], preferred_element_type=F32)
    if final:
        g_ref, out_ref = rest
        out_ref[...] = _rms(x_ref[...] + mod_ref[...][:, 2 * d:] * y, g_ref[...])
    else:
        modc_ref, out_ref = rest
        m_head, m_lat = _segment_mods(mod_ref, modc_ref, blocks_per_sample)
        out_ref[...] = x_ref[...] + _two_segments(y, n_ctx, lambda v: v * m_head[:, 2 * d:], lambda v: v * m_lat[:, 2 * d:])


def _outproj_odd(att, xs2, mod5, layer, w, o, tm, blocks_per_sample, bsz, n_ctx=None, final_g=None):
    r, d = xs2.shape
    wspec = pl.BlockSpec((None, w.shape[1], d), lambda *_: (o, 0, 0))
    if final_g is None:
        blk = lambda n: pl.BlockSpec((tm, n), lambda i: (i, 0))
        mod_seg = lambda seg: pl.BlockSpec((None, None, None, 1, 3 * d),
                                           lambda i: (layer, seg, i // blocks_per_sample, 0, 0))
        return pl.pallas_call(
            functools.partial(_outproj_odd_kernel, final=False, n_ctx=n_ctx, blocks_per_sample=blocks_per_sample),
            grid=(r // tm,),
            in_specs=[blk(att.shape[1]), blk(d), mod_seg(0), wspec, mod_seg(1)],
            out_specs=blk(d),
            out_shape=jax.ShapeDtypeStruct((r, d), F32),
            compiler_params=_cparams(40, "parallel"),
            name="outproj_odd",
        )(att, xs2, mod5, w, mod5)
    nlat = blocks_per_sample - 1
    blk = lambda n: pl.BlockSpec((tm, n), lambda b, j: (b * blocks_per_sample + 1 + j, 0))
    return pl.pallas_call(
        functools.partial(_outproj_odd_kernel, final=True),
        grid=(bsz, nlat),
        in_specs=[blk(att.shape[1]), blk(d),
                  pl.BlockSpec((None, None, None, 1, 3 * d), lambda b, j: (layer, 0, b, 0, 0)),
                  wspec, pl.BlockSpec((1, d), lambda b, j: (0, 0))],
        out_specs=pl.BlockSpec((None, tm, d), lambda b, j: (b, j, 0)),
        out_shape=jax.ShapeDtypeStruct((bsz, nlat * tm, d), F32),
        compiler_params=_cparams(32, "parallel", "parallel"),
        name="outproj_final",
    )(att, xs2, mod5, w, final_g.reshape(1, d))


def _rope_fold(v, tab):
    w = v * tab
    return w + pltpu.roll(w, LANE // 2, axis=1)


def _segment_mods(mod_lat_ref, mod_ctx_ref, blocks_per_sample):
    first = pl.program_id(0) % blocks_per_sample == 0
    m_lat = mod_lat_ref[...]
    return jnp.where(first, mod_ctx_ref[...], m_lat), m_lat


def _two_segments(y, n_head, f_head, f_rest):
    if n_head == y.shape[0]:
        return f_head(y)
    return jnp.concatenate([f_head(y[:n_head]), f_rest(y[n_head:])], axis=0)


def _mla_in_kernel(x_ref, mod_ref, modc_ref, g_ref, w_ref, tab_ref, qg_ref, kvg_ref, wq_ref, wkv_ref,
                   gate_ref, q_ref, k_ref, v_ref, *, q_rank, kv_rank, n_ctx, blocks_per_sample):
    heads = q_ref.shape[0]
    d = x_ref.shape[1]
    m_head, m_lat = _segment_mods(mod_ref, modc_ref, blocks_per_sample)
    y = _rms(x_ref[...], g_ref[...])
    shift_scale = lambda m: (lambda v: v * (1.0 + m[:, d:2 * d]) + m[:, :d])
    n = _two_segments(y, n_ctx, shift_scale(m_head), shift_scale(m_lat)).astype(BF16)
    proj = lambda lo, hi: jnp.dot(n, w_ref[:, lo:hi], preferred_element_type=F32)
    off_g = q_rank + kv_rank
    off_kr = off_g + gate_ref.shape[1]
    gate_ref[...] = proj(off_g, off_kr).astype(gate_ref.dtype)
    tab = tab_ref[...]
    q = jnp.dot(_rms(proj(0, q_rank), qg_ref[...]).astype(BF16), wq_ref[...], preferred_element_type=F32)
    kv = jnp.dot(_rms(proj(q_rank, off_g), kvg_ref[...]).astype(BF16), wkv_ref[...], preferred_element_type=F32)
    kr = _rope_fold(proj(off_kr, off_kr + 2 * MLA_ROPE), tab)[:, :MLA_ROPE].astype(BF16)
    per = MLA_NOPE + 2 * MLA_ROPE
    for h in range(heads):
        qh = q[:, h * per:(h + 1) * per]
        q_ref[h, :, 0:MLA_NOPE] = (qh[:, :MLA_NOPE] * Q_PRESCALE).astype(BF16)
        qr = _rope_fold(qh[:, MLA_NOPE:], tab)[:, :MLA_ROPE] * Q_PRESCALE
        q_ref[h, :, MLA_NOPE:MLA_NOPE + MLA_ROPE] = qr.astype(BF16)
        kvh = kv[:, h * (MLA_NOPE + MLA_V):(h + 1) * (MLA_NOPE + MLA_V)]
        k_ref[h, :, 0:MLA_NOPE] = kvh[:, :MLA_NOPE].astype(BF16)
        k_ref[h, :, MLA_NOPE:MLA_NOPE + MLA_ROPE] = kr
        v_ref[h] = kvh[:, MLA_NOPE:].astype(BF16)


def _mla_in(xs2, mod5, norm_g3, w_in, tab, q_norm, kv_norm, wq, wkv, layer, o, tm, blocks_per_sample, n_ctx):
    r, d = xs2.shape
    heads = MLA_HEADS
    dk = MLA_NOPE + MLA_ROPE
    q_rank, kv_rank = q_norm.shape[-1], kv_norm.shape[-1]
    rows = lambda n: pl.BlockSpec((tm, n), lambda i: (i, 0))
    sel = lambda a, idx: pl.BlockSpec((None,) + a.shape[1:], lambda i: (idx, 0, 0))
    hrows = lambda n: pl.BlockSpec((heads, tm, n), lambda i: (0, i, 0))
    mod_seg = lambda seg: pl.BlockSpec((None, None, None, 1, 3 * d),
                                       lambda i: (layer, seg, i // blocks_per_sample, 0, 0))
    return pl.pallas_call(
        functools.partial(_mla_in_kernel, q_rank=q_rank, kv_rank=kv_rank, n_ctx=n_ctx,
                          blocks_per_sample=blocks_per_sample),
        grid=(r // tm,),
        in_specs=[rows(d), mod_seg(0), mod_seg(1), sel(norm_g3, layer), sel(w_in, o),
                  pl.BlockSpec((tm, LANE), lambda i: (i % blocks_per_sample, 0)),
                  sel(q_norm, o), sel(kv_norm, o), sel(wq, o), sel(wkv, o)],
        out_specs=[rows(heads * MLA_V), hrows(dk), hrows(dk), hrows(MLA_V)],
        out_shape=[jax.ShapeDtypeStruct((r, heads * MLA_V), BF16),
                   jax.ShapeDtypeStruct((heads, r, dk), BF16), jax.ShapeDtypeStruct((heads, r, dk), BF16),
                   jax.ShapeDtypeStruct((heads, r, MLA_V), BF16)],
        compiler_params=_cparams(56, "parallel"),
        name="mla_in",
    )(xs2, mod5, mod5, norm_g3, w_in, tab, q_norm, kv_norm, wq, wkv)


def _attn_kernel(q_ref, k_ref, v_ref, g_ref, o_ref, vx_ref, s0_ref, s1_ref, *, with_ctx, n_ctx, tq):
    per = k_ref.shape[0]
    n_lat = (per - n_ctx) // tq
    vx_ref[:, 0:MLA_V] = v_ref[...]
    vx_ref[:, MLA_V:] = jnp.ones((per, MLA_V), BF16)

    def scores(r0, nq, s_ref, nk):
        s_ref[0:nq, 0:nk] = lax.dot_general(q_ref[r0:r0 + nq, :], k_ref[0:nk, :], (((1,), (1,)), ((), ())),
                                            preferred_element_type=F32)

    def finish(r0, nq, s_ref, nk):
        s = s_ref[0:nq, 0:nk]
        p = jnp.exp2(s - jnp.max(s, axis=-1, keepdims=True))
        ox = jnp.dot(p.astype(BF16), vx_ref[0:nk, :], preferred_element_type=F32)
        o = ox[:, :MLA_V] / ox[:, MLA_V:]
        g = g_ref[r0:r0 + nq, :].astype(F32)
        o_ref[r0:r0 + nq, :] = (o * _silu(g)).astype(o_ref.dtype)

    if with_ctx:
        scores(0, n_ctx, s0_ref, n_ctx)
        finish(0, n_ctx, s0_ref, n_ctx)
    else:
        o_ref[0:n_ctx, :] = jnp.zeros((n_ctx, MLA_V), o_ref.dtype)

    s_refs = (s0_ref, s1_ref)
    scores(n_ctx, tq, s0_ref, per)
    for j in range(n_lat):
        if j + 1 < n_lat:
            scores(n_ctx + (j + 1) * tq, tq, s_refs[(j + 1) % 2], per)
        finish(n_ctx + j * tq, tq, s_refs[j % 2], per)


def _attention(q, k, v, gate, bsz, n_ctx, with_ctx):
    heads, r, dk = q.shape
    per = r // bsz
    tq = min(ATTN_Q_ROWS, per - n_ctx)
    assert (per - n_ctx) % tq == 0 and n_ctx <= tq
    blk = lambda n: pl.BlockSpec((None, per, n), lambda b, h: (h, b, 0))
    return pl.pallas_call(
        functools.partial(_attn_kernel, with_ctx=with_ctx, n_ctx=n_ctx, tq=tq),
        grid=(bsz, heads),
        in_specs=[blk(dk), blk(dk), blk(MLA_V), pl.BlockSpec((per, MLA_V), lambda b, h: (b, h))],
        out_specs=pl.BlockSpec((per, MLA_V), lambda b, h: (b, h)),
        out_shape=jax.ShapeDtypeStruct((r, heads * MLA_V), BF16),
        scratch_shapes=[pltpu.VMEM((per, 2 * MLA_V), BF16), pltpu.VMEM((tq, per), F32), pltpu.VMEM((tq, per), F32)],
        compiler_params=_cparams(56, "parallel", "parallel"),
        name="mla_attention",
    )(q, k, v, gate)


def _s5_matrices(a_re, a_im, bb_re, bb_im, c_re, c_im):
    gps = LANE // S5_GROUP
    ne, nd, g, p, h = bb_re.shape
    slabs = g // gps
    eye = jnp.eye(gps, dtype=F32)
    bd_in = lambda m: (jnp.swapaxes(m.reshape(ne, nd, slabs, gps, p, h), -1, -2)[..., None, :]
                       * eye[:, None, :, None]).reshape(ne, nd, slabs, gps * h, gps * p)
    bd_out = lambda m: (jnp.transpose(m.reshape(ne, nd, slabs, gps, h, p), (0, 1, 2, 5, 3, 4))[:, :, :, None]
                        * eye[:, None, :, None]).reshape(ne, nd, slabs, gps * p, gps * h)
    bm = jnp.concatenate([bd_in(bb_re), bd_in(bb_im)], axis=-1).astype(BF16)
    cm = jnp.concatenate([bd_out(c_re), -bd_out(c_im)], axis=-2).astype(BF16)
    ar = a_re.reshape(ne, nd, slabs, 1, gps * p)
    ai = a_im.reshape(ne, nd, slabs, 1, gps * p)
    return bm, cm, ar, ai


def _rope_table(n_ctx, n_lat):
    axis = MLA_ROPE // 2
    rows = n_lat // GRID_W
    row = jnp.repeat(jnp.arange(rows, dtype=F32), GRID_W)
    col = jnp.tile(jnp.arange(GRID_W, dtype=F32), rows)
    inv = ROPE_BASE ** (-jnp.arange(0, axis, 2, dtype=F32) / axis)
    ang = jnp.concatenate([row[:, None] * inv, col[:, None] * inv], axis=-1)
    cos, sin = jnp.cos(ang), jnp.sin(ang)
    lat = jnp.concatenate([cos, cos, -sin, sin], axis=-1)
    one, zero = jnp.ones((n_ctx, axis), F32), jnp.zeros((n_ctx, axis), F32)
    return jnp.concatenate([jnp.concatenate([one, one, zero, zero], axis=-1), lat], axis=0)


def _swap_halves(w):
    half = w.shape[-1] // 2
    return jnp.concatenate([w[..., half:], w[..., :half]], axis=-1)


def kernel(x, c, ctx, c_ctx, norm_g, mod_w, mod_b, ev_w_in, lru_conv_w, lru_conv_b, lru_wr, lru_br, lru_wi, lru_bi, lru_lam, s5_lam_re, s5_lam_im, s5_log_dt, s5_b_re, s5_b_im, s5_c_re, s5_c_im, s5_d, s5_glu_w, s5_glu_b, ev_w_out, mla_w_in, mla_q_norm, mla_w_uq, mla_kv_norm, mla_w_ukv, mla_w_out, final_g):
    bsz, n_lat, d = x.shape
    n_ctx = ctx.shape[1]
    steps = n_ctx + n_lat
    depth = mod_w.shape[0]
    n_even, n_odd = ev_w_in.shape[0], mla_w_in.shape[0]
    assert bsz == SUBLANE, "time-major layout maps the batch onto the sublanes of one vreg"
    assert depth % 2 == 0, "the last layer is an attention layer"
    lru_w = lru_conv_w.shape[-1]
    s5_w = s5_d.shape[-2] * s5_d.shape[-1]
    nh = lru_w // LRU_HEAD_DIM
    heads = MLA_HEADS
    q_rank, kv_rank = mla_q_norm.shape[-1], mla_kv_norm.shape[-1]

    cond = jnp.concatenate([c, jnp.broadcast_to(c_ctx[None], (bsz, d))], axis=0)
    mod5 = _modulation(cond, mod_w, mod_b).reshape(depth, 2, bsz, 1, 3 * d)
    norm_g3 = norm_g.reshape(depth, 1, d)

    ev_w_in_b = ev_w_in.astype(BF16)
    ev_w_out_b = ev_w_out.astype(BF16)
    glu_w_b = s5_glu_w.astype(BF16)
    glu_b3 = s5_glu_b.reshape(n_even, 1, s5_w)
    conv_b3 = lru_conv_b.reshape(n_even, 1, lru_w)
    wg = (0.5 * jnp.concatenate([lru_wr, lru_wi], axis=-1)).astype(BF16)
    bg = 0.5 * jnp.concatenate([lru_br.reshape(n_even, 2, nh, LRU_HEAD_DIM), lru_bi.reshape(n_even, 2, nh, LRU_HEAD_DIM)],
                               axis=-1).reshape(n_even, 2, 1, 2 * lru_w)
    lam4 = lru_lam.reshape(n_even, 2, 1, lru_w)
    a_re, a_im, bb_re, bb_im = _s5_discretise(s5_lam_re, s5_lam_im, s5_log_dt, s5_b_re, s5_b_im)
    s5_bm, s5_cm, s5_ar, s5_ai = _s5_matrices(a_re, a_im, bb_re, bb_im, s5_c_re, s5_c_im)
    dsk3 = s5_d.reshape(n_even, 1, s5_w)

    off_kr, off_g = q_rank + kv_rank, q_rank + kv_rank + MLA_ROPE
    w_kr = mla_w_in[..., off_kr:off_g]
    mla_w_in_b = jnp.concatenate([mla_w_in[..., :off_kr], mla_w_in[..., off_g:], w_kr, _swap_halves(w_kr)],
                                 axis=-1).astype(BF16)
    wq = mla_w_uq.reshape(n_odd, q_rank, heads, MLA_NOPE + MLA_ROPE)
    wq_b = jnp.concatenate([wq, _swap_halves(wq[..., MLA_NOPE:])], axis=-1).reshape(n_odd, q_rank, -1).astype(BF16)
    wkv_b = mla_w_ukv.astype(BF16)
    mla_w_out_b = mla_w_out.astype(BF16)
    q_norm3 = mla_q_norm.reshape(n_odd, 1, q_rank)
    kv_norm3 = mla_kv_norm.reshape(n_odd, 1, kv_rank)
    rope_tab = _rope_table(n_ctx, n_lat)

    ts = PROJ_STEPS
    ctx_blocks = n_ctx // ts
    seg_tm = lambda i: (i < ctx_blocks).astype(jnp.int32)
    tq = n_ctx
    blocks_per_sample = steps // tq
    rows = bsz * steps
    tm_odd = ODD_ROWS if steps % ODD_ROWS == 0 and n_ctx < ODD_ROWS else tq
    assert steps % tm_odd == 0 and n_ctx <= tm_odd

    res = (ctx, x)
    for l in range(depth):
        if l % 2 == 0:
            e = l // 2
            xa, ga, ub, gb = _inproj(res, mod5, norm_g3, ev_w_in_b, l, e, seg_tm,
                                     (lru_w, lru_w, s5_w, s5_w), (BF16,) * 4, ts, ctx_blocks, steps)
            ya = _lru_mix(xa, ga, lru_conv_w, conv_b3, wg, bg, lam4, e, n_ctx * bsz)
            yg = _s5_mix(ub, s5_bm, s5_cm, s5_ar, s5_ai, dsk3, e, n_ctx * bsz)
            xs = _outproj_even(ya, yg, gb, res, mod5, l, seg_tm, glu_w_b, glu_b3, ev_w_out_b, e, ts, ctx_blocks, steps)
            res = (xs,)
        else:
            o = l // 2
            with_ctx = l < depth - 1
            xs2 = xs.reshape(rows, d)
            gate, q, k, v = _mla_in(xs2, mod5, norm_g3, mla_w_in_b, rope_tab, q_norm3, kv_norm3, wq_b, wkv_b,
                                    l, o, tm_odd, steps // tm_odd, n_ctx)
            att = _attention(q, k, v, gate, bsz, n_ctx, with_ctx)
            if with_ctx:
                xs = _outproj_odd(att, xs2, mod5, l, mla_w_out_b, o, tm_odd, steps // tm_odd, bsz,
                                  n_ctx=n_ctx).reshape(bsz, steps, d)
                res = (xs,)
            else:
                xs = _outproj_odd(att, xs2, mod5, l, mla_w_out_b, o, tq, blocks_per_sample, bsz, final_g=final_g)
    return xs
```

```python
import functools
import math

import jax
import jax.numpy as jnp
from jax import lax
from jax.experimental import pallas as pl
from jax.experimental.pallas import tpu as pltpu

F32 = jnp.float32
BF16 = jnp.bfloat16

NORM_EPS = 1e-6
LRU_C = 8.0
LRU_HEAD_DIM = 128
S5_GROUP = 16
MLA_HEADS = 8
MLA_NOPE = 128
MLA_ROPE = 64
MLA_V = 128
MLA_SCALE = 1.0 / math.sqrt(MLA_NOPE + MLA_ROPE)
LOG2E = math.log2(math.e)
Q_PRESCALE = MLA_SCALE * LOG2E
ROPE_BASE = 10000.0
GRID_W = 64

LANE = 128
SUBLANE = 8
SCAN_STEPS = 32
PROJ_STEPS = 128
ODD_ROWS = 768
ATTN_Q_ROWS = 1024
MIB = 1024 * 1024


def _cparams(vmem_mib, *sem):
    return pltpu.CompilerParams(dimension_semantics=tuple(sem), vmem_limit_bytes=int(vmem_mib * MIB))


def _sigmoid(x):
    return 0.5 * jnp.tanh(0.5 * x) + 0.5


def _silu(x):
    return x * _sigmoid(x)


def _sqrt_nonneg(x):
    return jnp.where(x > 0.0, x * lax.rsqrt(x), 0.0)


def _rms(x, g):
    return x * lax.rsqrt(jnp.mean(x * x, axis=-1, keepdims=True) + NORM_EPS) * g


def _mod_kernel(c_ref, w_ref, b_ref, o_ref):
    a = _silu(c_ref[...])
    o_ref[...] = jnp.dot(a, w_ref[...], preferred_element_type=F32, precision=lax.Precision.HIGHEST) + b_ref[...]


def _modulation(cond, mod_w, mod_b):
    depth, d, n = mod_w.shape
    rows = cond.shape[0]
    tn = min(n, 1024)
    return pl.pallas_call(
        _mod_kernel,
        grid=(depth, n // tn),
        in_specs=[
            pl.BlockSpec((rows, d), lambda l, j: (0, 0)),
            pl.BlockSpec((None, d, tn), lambda l, j: (l, 0, j)),
            pl.BlockSpec((None, 1, tn), lambda l, j: (l, 0, j)),
        ],
        out_specs=pl.BlockSpec((None, rows, tn), lambda l, j: (l, 0, j)),
        out_shape=jax.ShapeDtypeStruct((depth, rows, n), F32),
        compiler_params=_cparams(32, "parallel", "parallel"),
        name="modulation",
    )(cond, mod_w, mod_b.reshape(depth, 1, n))


def _residual_specs(xs, ts, ctx_blocks):
    bsz, _, d = xs[0].shape
    if len(xs) == 1:
        return [pl.BlockSpec((bsz, ts, d), lambda i: (0, i, 0))]
    return [pl.BlockSpec((bsz, ts, d), lambda i: (0, jnp.minimum(i, ctx_blocks - 1), 0)),
            pl.BlockSpec((bsz, ts, d), lambda i: (0, jnp.maximum(i - ctx_blocks, 0), 0))]


def _residual_rows(x_refs, ctx_blocks):
    if len(x_refs) == 1:
        return x_refs[0][...]
    return jnp.where(pl.program_id(0) < ctx_blocks, x_refs[0][...], x_refs[1][...])


def _inproj_kernel(*refs, splits, n_res, ctx_blocks):
    x_refs, (mod_ref, g_ref, w_ref), out_refs = refs[:n_res], refs[n_res:n_res + 3], refs[n_res + 3:]
    x = _residual_rows(x_refs, ctx_blocks)
    d = x.shape[-1]
    m = mod_ref[...]
    n = _rms(x, g_ref[...]) * (1.0 + m[..., d:2 * d]) + m[..., :d]
    n = jnp.swapaxes(n, 0, 1).reshape(-1, d).astype(BF16)
    off = 0
    for o_ref, width in zip(out_refs, splits):
        o_ref[...] = jnp.dot(n, w_ref[:, off:off + width], preferred_element_type=F32).astype(o_ref.dtype)
        off += width


def _inproj(xs, mod5, g3, w3, layer, e, seg, splits, dtypes, ts, ctx_blocks, steps):
    bsz, _, d = xs[0].shape
    n = w3.shape[2]
    tm = ts * bsz
    return pl.pallas_call(
        functools.partial(_inproj_kernel, splits=splits, n_res=len(xs), ctx_blocks=ctx_blocks),
        grid=(steps // ts,),
        in_specs=_residual_specs(xs, ts, ctx_blocks) + [
            pl.BlockSpec((None, None, bsz, 1, 3 * d), lambda i: (layer, seg(i), 0, 0, 0)),
            pl.BlockSpec((None, 1, d), lambda i: (layer, 0, 0)),
            pl.BlockSpec((None, d, n), lambda i: (e, 0, 0))],
        out_specs=[pl.BlockSpec((tm, s), lambda i: (i, 0)) for s in splits],
        out_shape=[jax.ShapeDtypeStruct((steps * bsz, s), dt) for s, dt in zip(splits, dtypes)],
        compiler_params=_cparams(58, "parallel"),
        name="inproj",
    )(*xs, mod5, g3, w3)


def _softplus(x):
    return jnp.log1p(jnp.exp(-jnp.abs(x))) + jnp.maximum(x, 0.0)


def _chunk_order(j, n_ctx, n_all):
    return jnp.where(j < n_ctx, n_ctx - 1 - j, n_all - 1 - (j - n_ctx))


def _lru_kernel(xa_ref, ga_ref, cw_ref, cb_ref, wg_ref, bg_ref, lam_ref, o_ref,
                hf_ref, hb_ref, *ab_refs, n_ctx, n_all, ch):
    rows_total = xa_ref.shape[0]
    hd = xa_ref.shape[1]
    tb = ch // SUBLANE
    halo = 2 * SUBLANE
    row = lax.broadcasted_iota(jnp.int32, (ch, hd), 0)
    cw = cw_ref[...]
    cb = cb_ref[...]
    bufs = ((ab_refs[0:2], ab_refs[2:4]), (ab_refs[4:6], ab_refs[6:8]))
    dst = (hf_ref, hb_ref)
    order = (lambda j: j, lambda j: _chunk_order(j, n_ctx, n_all))
    k_log2a = [(-0.5 * LRU_C * LOG2E) * _softplus(-lam_ref[d]) for d in (0, 1)]

    def conv(c):
        r0 = pl.multiple_of(c * ch, ch)
        first = jnp.logical_or(c == 0, c == n_ctx)
        last = jnp.logical_or(c == n_ctx - 1, c == n_all - 1)
        rp = pl.multiple_of(jnp.maximum(r0 - halo, 0), halo)
        rn = pl.multiple_of(jnp.minimum(r0 + ch, rows_total - halo), halo)
        ext = jnp.concatenate([xa_ref[pl.ds(rp, halo), :], xa_ref[pl.ds(r0, ch), :],
                               xa_ref[pl.ds(rn, halo), :]], axis=0).astype(F32)
        xm2 = jnp.where(jnp.logical_and(first, row < 2 * SUBLANE), 0.0, ext[0:ch])
        xm1 = jnp.where(jnp.logical_and(first, row < SUBLANE), 0.0, ext[SUBLANE:SUBLANE + ch])
        x0 = ext[2 * SUBLANE:2 * SUBLANE + ch]
        xp1 = jnp.where(jnp.logical_and(last, row >= ch - SUBLANE), 0.0, ext[3 * SUBLANE:3 * SUBLANE + ch])
        return cb + cw[0:1] * xm2 + cw[1:2] * xm1 + cw[2:3] * x0 + cw[3:4] * xp1

    def coeffs(c, d, a_ref, b_ref):
        u = conv(c)
        t = jnp.tanh(jnp.dot(u.astype(BF16), wg_ref[d], preferred_element_type=F32) + bg_ref[d])
        a = jnp.exp2(t[:, :hd] * k_log2a[d] + k_log2a[d])
        a_ref[...] = a
        hu = 0.5 * u
        b_ref[...] = _sqrt_nonneg(1.0 - a * a) * (t[:, hd:] * hu + hu)

    def scan(c, d, a_ref, b_ref, h):
        r0 = pl.multiple_of(c * ch, ch)
        for t in range(tb):
            rr = (t if d == 0 else tb - 1 - t) * SUBLANE
            h = a_ref[rr:rr + SUBLANE, :] * h + b_ref[rr:rr + SUBLANE, :]
            dst[d][pl.ds(r0 + rr, SUBLANE), :] = h
        return h

    for d in (0, 1):
        coeffs(order[d](0), d, *bufs[d][0])

    def chunk_pair(i, hs):
        hs = list(hs)
        for slot in (0, 1):
            j = 2 * i + slot
            nxt = jnp.minimum(j + 1, n_all - 1)
            for d in (0, 1):
                coeffs(order[d](nxt), d, *bufs[d][1 - slot])
            for d in (0, 1):
                hs[d] = scan(order[d](j), d, *bufs[d][slot], hs[d])
        return tuple(hs)

    h0 = jnp.zeros((SUBLANE, hd), F32)
    lax.fori_loop(0, n_all // 2, chunk_pair, (h0, h0))

    def finish(c, carry):
        r0 = pl.multiple_of(c * ch, ch)
        g = ga_ref[pl.ds(r0, ch), :].astype(F32)
        o_ref[pl.ds(r0, ch), :] = ((hf_ref[pl.ds(r0, ch), :] + hb_ref[pl.ds(r0, ch), :]) * _silu(g)).astype(o_ref.dtype)
        return carry

    lax.fori_loop(0, n_all, finish, 0)


def _lru_mix(xa, ga, conv_w, conv_b, wg, bg, lam, e, n_ctx_rows):
    r, w = xa.shape
    hd = LRU_HEAD_DIM
    ch = SCAN_STEPS * SUBLANE
    assert (r // ch) % 2 == 0, "chunks are processed in pairs"
    kern = functools.partial(_lru_kernel, n_ctx=n_ctx_rows // ch, n_all=r // ch, ch=ch)
    return pl.pallas_call(
        kern,
        grid=(w // hd,),
        in_specs=[
            pl.BlockSpec((r, hd), lambda h: (0, h)),
            pl.BlockSpec((r, hd), lambda h: (0, h)),
            pl.BlockSpec((None, conv_w.shape[1], hd), lambda h: (e, 0, h)),
            pl.BlockSpec((None, 1, hd), lambda h: (e, 0, h)),
            pl.BlockSpec((None, 2, None, hd, 2 * hd), lambda h: (e, 0, h, 0, 0)),
            pl.BlockSpec((None, 2, 1, 2 * hd), lambda h: (e, 0, 0, h)),
            pl.BlockSpec((None, 2, 1, hd), lambda h: (e, 0, 0, h)),
        ],
        out_specs=pl.BlockSpec((r, hd), lambda h: (0, h)),
        out_shape=jax.ShapeDtypeStruct((r, w), BF16),
        scratch_shapes=[pltpu.VMEM((r, hd), F32)] * 2 + [pltpu.VMEM((ch, hd), F32)] * 8,
        compiler_params=_cparams(56, "parallel"),
        name="lru_mix",
    )(xa, ga, conv_w, conv_b, wg, bg, lam)


def _s5_disc_kernel(lre_ref, lim_ref, ldt_ref, bre_ref, bim_ref, are_ref, aim_ref, ore_ref, oim_ref):
    lam_re = lre_ref[...]
    lam_im = lim_ref[...]
    dt = jnp.exp(ldt_ref[...])
    mag = jnp.exp(lam_re * dt)
    ab_re = mag * jnp.cos(lam_im * dt)
    ab_im = mag * jnp.sin(lam_im * dt)
    den = lam_re * lam_re + lam_im * lam_im
    nr = ab_re - 1.0
    f_re = (nr * lam_re + ab_im * lam_im) / den
    f_im = (ab_im * lam_re - nr * lam_im) / den
    b_re = bre_ref[...]
    b_im = bim_ref[...]
    are_ref[...] = ab_re
    aim_ref[...] = ab_im
    ore_ref[...] = f_re * b_re - f_im * b_im
    oim_ref[...] = f_re * b_im + f_im * b_re


def _s5_discretise(lam_re, lam_im, log_dt, b_re, b_im):
    lead = b_re.shape[:-2]
    p, h = b_re.shape[-2:]
    nrow = math.prod(lead)
    flat = lambda v: jnp.broadcast_to(v[..., None], b_re.shape).reshape(nrow, p * h)
    shp = jax.ShapeDtypeStruct((nrow, p * h), F32)
    a_re, a_im, o_re, o_im = pl.pallas_call(
        _s5_disc_kernel, out_shape=[shp] * 4, name="s5_discretise",
    )(flat(lam_re), flat(lam_im), flat(log_dt), b_re.reshape(nrow, p * h), b_im.reshape(nrow, p * h))
    unflat = lambda v: v.reshape(*lead, p, h)
    return unflat(a_re)[..., 0], unflat(a_im)[..., 0], unflat(o_re), unflat(o_im)


def _s5_kernel(u_ref, bm_ref, cm_ref, ar_ref, ai_ref, dsk_ref, o_ref, yf_ref, yb_ref, *v_refs, n_ctx, n_all, ch):
    ns = ar_ref.shape[-1]
    tb = ch // SUBLANE
    bufs = (v_refs[0:2], v_refs[2:4])
    h_refs = (v_refs[4:6], v_refs[6:8])
    y_refs = (yf_ref, yb_ref)
    order = (lambda j: j, lambda j: _chunk_order(j, n_ctx, n_all))
    row0 = lambda j, d: pl.multiple_of(order[d](j) * ch, ch)

    def expand(j, slot):
        for d in (0, 1):
            bufs[d][slot][...] = jnp.dot(u_ref[pl.ds(row0(j, d), ch), :], bm_ref[d], preferred_element_type=F32)

    def scan(slot, hs):
        hs = [list(hs[0:2]), list(hs[2:4])]
        ar = [jnp.broadcast_to(ar_ref[d], (SUBLANE, ns)) for d in (0, 1)]
        ai = [jnp.broadcast_to(ai_ref[d], (SUBLANE, ns)) for d in (0, 1)]
        for k in range(tb // 2):
            for d in (0, 1):
                v_ref = bufs[d][slot]
                pair = (2 * k, 2 * k + 1) if d == 0 else (tb - 1 - 2 * k, tb - 2 - 2 * k)
                new = {}
                for t in pair:
                    rr = t * SUBLANE
                    hr, hi = hs[d]
                    nr = ar[d] * hr - ai[d] * hi + v_ref[rr:rr + SUBLANE, 0:ns]
                    ni = ar[d] * hi + ai[d] * hr + v_ref[rr:rr + SUBLANE, ns:2 * ns]
                    hs[d] = new[t] = [nr, ni]
                lo = min(pair)
                rows = slice(lo * SUBLANE, (lo + 2) * SUBLANE)
                for part in (0, 1):
                    both = jnp.concatenate([new[lo][part], new[lo + 1][part]], axis=0)
                    h_refs[d][slot][rows, part * ns:(part + 1) * ns] = both.astype(BF16)
        return (*hs[0], *hs[1])

    def project(j, slot):
        for d in (0, 1):
            y_refs[d][pl.ds(row0(j, d), ch), :] = jnp.dot(h_refs[d][slot][...], cm_ref[d], preferred_element_type=F32)

    last = n_all - 1
    z = jnp.zeros((SUBLANE, ns), F32)
    expand(0, 0)
    hs = scan(0, (z, z, z, z))
    expand(1, 1)

    def chunk_pair(i, hs):
        j = 2 * i
        expand(jnp.minimum(j + 2, last), 0)
        project(j, 0)
        hs = scan(1, hs)
        expand(jnp.minimum(j + 3, last), 1)
        project(j + 1, 1)
        hs = scan(0, hs)
        return hs

    lax.fori_loop(0, n_all // 2, chunk_pair, hs)

    def finish(c, carry):
        r0 = pl.multiple_of(c * ch, ch)
        u = u_ref[pl.ds(r0, ch), :].astype(F32)
        y = yf_ref[pl.ds(r0, ch), :] + yb_ref[pl.ds(r0, ch), :] + dsk_ref[...] * u
        o_ref[pl.ds(r0, ch), :] = jax.nn.gelu(y).astype(o_ref.dtype)
        return carry

    lax.fori_loop(0, n_all, finish, 0)


def _s5_mix(ub, bm, cm, ar, ai, dsk, e, n_ctx_rows):
    r, w = ub.shape
    ns = ar.shape[-1]
    ch = SCAN_STEPS * SUBLANE
    assert (r // ch) % 2 == 0, "chunks are processed in pairs"
    kern = functools.partial(_s5_kernel, n_ctx=n_ctx_rows // ch, n_all=r // ch, ch=ch)
    return pl.pallas_call(
        kern,
        grid=(w // LANE,),
        in_specs=[
            pl.BlockSpec((r, LANE), lambda s: (0, s)),
            pl.BlockSpec((None, 2, None, LANE, 2 * ns), lambda s: (e, 0, s, 0, 0)),
            pl.BlockSpec((None, 2, None, 2 * ns, LANE), lambda s: (e, 0, s, 0, 0)),
            pl.BlockSpec((None, 2, None, 1, ns), lambda s: (e, 0, s, 0, 0)),
            pl.BlockSpec((None, 2, None, 1, ns), lambda s: (e, 0, s, 0, 0)),
            pl.BlockSpec((None, 1, LANE), lambda s: (e, 0, s)),
        ],
        out_specs=pl.BlockSpec((r, LANE), lambda s: (0, s)),
        out_shape=jax.ShapeDtypeStruct((r, w), BF16),
        scratch_shapes=([pltpu.VMEM((r, LANE), F32)] * 2 + [pltpu.VMEM((ch, 2 * ns), F32)] * 4
                        + [pltpu.VMEM((ch, 2 * ns), BF16)] * 4),
        compiler_params=_cparams(54, "parallel"),
        name="s5_mix",
    )(ub, bm, cm, ar, ai, dsk)


def _outproj_even_kernel(ya_ref, yg_ref, gb_ref, *refs, n_res, ctx_blocks):
    x_refs, (mod_ref, gw_ref, gbias_ref, w1_ref, w2_ref, o_ref) = refs[:n_res], refs[n_res:]
    bsz, ts, d = x_refs[0].shape
    yg = yg_ref[...]
    z = jnp.dot(yg, gw_ref[...], preferred_element_type=F32) + gbias_ref[...]
    yb = yg.astype(F32) * _sigmoid(z) * _silu(gb_ref[...].astype(F32))
    o = jnp.dot(ya_ref[...], w1_ref[...], preferred_element_type=F32)
    o = o + jnp.dot(yb.astype(BF16), w2_ref[...], preferred_element_type=F32)
    o = jnp.swapaxes(o.reshape(ts, bsz, d), 0, 1)
    o_ref[...] = _residual_rows(x_refs, ctx_blocks) + mod_ref[...][..., 2 * d:] * o


def _outproj_even(ya, yg, gb, xs, mod5, layer, seg, glu_w, glu_b, w_out, e, ts, ctx_blocks, steps):
    bsz, _, d = xs[0].shape
    wa, wb = ya.shape[1], yg.shape[1]
    tm = ts * bsz
    rows = lambda n: pl.BlockSpec((tm, n), lambda i: (i, 0))
    return pl.pallas_call(
        functools.partial(_outproj_even_kernel, n_res=len(xs), ctx_blocks=ctx_blocks),
        grid=(steps // ts,),
        in_specs=[rows(wa), rows(wb), rows(wb)] + _residual_specs(xs, ts, ctx_blocks) + [
                  pl.BlockSpec((None, None, bsz, 1, 3 * d), lambda i: (layer, seg(i), 0, 0, 0)),
                  pl.BlockSpec((None, wb, wb), lambda i: (e, 0, 0)),
                  pl.BlockSpec((None, 1, wb), lambda i: (e, 0, 0)),
                  pl.BlockSpec((None, wa, d), lambda i: (e, 0, 0)),
                  pl.BlockSpec((None, wb, d), lambda i: (e, wa // wb, 0))],
        out_specs=pl.BlockSpec((bsz, ts, d), lambda i: (0, i, 0)),
        out_shape=jax.ShapeDtypeStruct((bsz, steps, d), F32),
        compiler_params=_cparams(58, "parallel"),
        name="outproj_even",
    )(ya, yg, gb, *xs, mod5, glu_w, glu_b, w_out, w_out)


def _outproj_odd_kernel(o_ref, x_ref, mod_ref, w_ref, *rest, final, n_ctx=None, blocks_per_sample=None):
    d = x_ref.shape[1]
    y = jnp.dot(o_ref[...], w_ref[...], preferred_element_type=F32)
    if final:
        g_ref, out_ref = rest
        out_ref[...] = _rms(x_ref[...] + mod_ref[...][:, 2 * d:] * y, g_ref[...])
    else:
        modc_ref, out_ref = rest
        m_head, m_lat = _segment_mods(mod_ref, modc_ref, blocks_per_sample)
        out_ref[...] = x_ref[...] + _two_segments(y, n_ctx, lambda v: v * m_head[:, 2 * d:], lambda v: v * m_lat[:, 2 * d:])


def _outproj_odd(att, xs2, mod5, layer, w, o, tm, blocks_per_sample, bsz, n_ctx=None, final_g=None):
    r, d = xs2.shape
    wspec = pl.BlockSpec((None, w.shape[1], d), lambda *_: (o, 0, 0))
    if final_g is None:
        blk = lambda n: pl.BlockSpec((tm, n), lambda i: (i, 0))
        mod_seg = lambda seg: pl.BlockSpec((None, None, None, 1, 3 * d),
                                           lambda i: (layer, seg, i // blocks_per_sample, 0, 0))
        return pl.pallas_call(
            functools.partial(_outproj_odd_kernel, final=False, n_ctx=n_ctx, blocks_per_sample=blocks_per_sample),
            grid=(r // tm,),
            in_specs=[blk(att.shape[1]), blk(d), mod_seg(0), wspec, mod_seg(1)],
            out_specs=blk(d),
            out_shape=jax.ShapeDtypeStruct((r, d), F32),
            compiler_params=_cparams(40, "parallel"),
            name="outproj_odd",
        )(att, xs2, mod5, w, mod5)
    nlat = blocks_per_sample - 1
    blk = lambda n: pl.BlockSpec((tm, n), lambda b, j: (b * blocks_per_sample + 1 + j, 0))
    return pl.pallas_call(
        functools.partial(_outproj_odd_kernel, final=True),
        grid=(bsz, nlat),
        in_specs=[blk(att.shape[1]), blk(d),
                  pl.BlockSpec((None, None, None, 1, 3 * d), lambda b, j: (layer, 0, b, 0, 0)),
                  wspec, pl.BlockSpec((1, d), lambda b, j: (0, 0))],
        out_specs=pl.BlockSpec((None, tm, d), lambda b, j: (b, j, 0)),
        out_shape=jax.ShapeDtypeStruct((bsz, nlat * tm, d), F32),
        compiler_params=_cparams(32, "parallel", "parallel"),
        name="outproj_final",
    )(att, xs2, mod5, w, final_g.reshape(1, d))


def _rope_fold(v, tab):
    w = v * tab
    return w + pltpu.roll(w, LANE // 2, axis=1)


def _segment_mods(mod_lat_ref, mod_ctx_ref, blocks_per_sample):
    first = pl.program_id(0) % blocks_per_sample == 0
    m_lat = mod_lat_ref[...]
    return jnp.where(first, mod_ctx_ref[...], m_lat), m_lat


def _two_segments(y, n_head, f_head, f_rest):
    if n_head == y.shape[0]:
        return f_head(y)
    return jnp.concatenate([f_head(y[:n_head]), f_rest(y[n_head:])], axis=0)


def _mla_in_kernel(x_ref, mod_ref, modc_ref, g_ref, w_ref, tab_ref, qg_ref, kvg_ref, wq_ref, wkv_ref,
                   gate_ref, q_ref, k_ref, v_ref, *, q_rank, kv_rank, n_ctx, blocks_per_sample):
    heads = q_ref.shape[0]
    d = x_ref.shape[1]
    m_head, m_lat = _segment_mods(mod_ref, modc_ref, blocks_per_sample)
    y = _rms(x_ref[...], g_ref[...])
    shift_scale = lambda m: (lambda v: v * (1.0 + m[:, d:2 * d]) + m[:, :d])
    n = _two_segments(y, n_ctx, shift_scale(m_head), shift_scale(m_lat)).astype(BF16)
    proj = lambda lo, hi: jnp.dot(n, w_ref[:, lo:hi], preferred_element_type=F32)
    off_g = q_rank + kv_rank
    off_kr = off_g + gate_ref.shape[1]
    gate_ref[...] = proj(off_g, off_kr).astype(gate_ref.dtype)
    tab = tab_ref[...]
    q = jnp.dot(_rms(proj(0, q_rank), qg_ref[...]).astype(BF16), wq_ref[...], preferred_element_type=F32)
    kv = jnp.dot(_rms(proj(q_rank, off_g), kvg_ref[...]).astype(BF16), wkv_ref[...], preferred_element_type=F32)
    kr = _rope_fold(proj(off_kr, off_kr + 2 * MLA_ROPE), tab)[:, :MLA_ROPE].astype(BF16)
    per = MLA_NOPE + 2 * MLA_ROPE
    for h in range(heads):
        qh = q[:, h * per:(h + 1) * per]
        q_ref[h, :, 0:MLA_NOPE] = (qh[:, :MLA_NOPE] * Q_PRESCALE).astype(BF16)
        qr = _rope_fold(qh[:, MLA_NOPE:], tab)[:, :MLA_ROPE] * Q_PRESCALE
        q_ref[h, :, MLA_NOPE:MLA_NOPE + MLA_ROPE] = qr.astype(BF16)
        kvh = kv[:, h * (MLA_NOPE + MLA_V):(h + 1) * (MLA_NOPE + MLA_V)]
        k_ref[h, :, 0:MLA_NOPE] = kvh[:, :MLA_NOPE].astype(BF16)
        k_ref[h, :, MLA_NOPE:MLA_NOPE + MLA_ROPE] = kr
        v_ref[h] = kvh[:, MLA_NOPE:].astype(BF16)


def _mla_in(xs2, mod5, norm_g3, w_in, tab, q_norm, kv_norm, wq, wkv, layer, o, tm, blocks_per_sample, n_ctx):
    r, d = xs2.shape
    heads = MLA_HEADS
    dk = MLA_NOPE + MLA_ROPE
    q_rank, kv_rank = q_norm.shape[-1], kv_norm.shape[-1]
    rows = lambda n: pl.BlockSpec((tm, n), lambda i: (i, 0))
    sel = lambda a, idx: pl.BlockSpec((None,) + a.shape[1:], lambda i: (idx, 0, 0))
    hrows = lambda n: pl.BlockSpec((heads, tm, n), lambda i: (0, i, 0))
    mod_seg = lambda seg: pl.BlockSpec((None, None, None, 1, 3 * d),
                                       lambda i: (layer, seg, i // blocks_per_sample, 0, 0))
    return pl.pallas_call(
        functools.partial(_mla_in_kernel, q_rank=q_rank, kv_rank=kv_rank, n_ctx=n_ctx,
                          blocks_per_sample=blocks_per_sample),
        grid=(r // tm,),
        in_specs=[rows(d), mod_seg(0), mod_seg(1), sel(norm_g3, layer), sel(w_in, o),
                  pl.BlockSpec((tm, LANE), lambda i: (i % blocks_per_sample, 0)),
                  sel(q_norm, o), sel(kv_norm, o), sel(wq, o), sel(wkv, o)],
        out_specs=[rows(heads * MLA_V), hrows(dk), hrows(dk), hrows(MLA_V)],
        out_shape=[jax.ShapeDtypeStruct((r, heads * MLA_V), BF16),
                   jax.ShapeDtypeStruct((heads, r, dk), BF16), jax.ShapeDtypeStruct((heads, r, dk), BF16),
                   jax.ShapeDtypeStruct((heads, r, MLA_V), BF16)],
        compiler_params=_cparams(56, "parallel"),
        name="mla_in",
    )(xs2, mod5, mod5, norm_g3, w_in, tab, q_norm, kv_norm, wq, wkv)


def _attn_kernel(q_ref, k_ref, v_ref, g_ref, o_ref, vx_ref, s0_ref, s1_ref, *, with_ctx, n_ctx, tq):
    per = k_ref.shape[0]
    n_lat = (per - n_ctx) // tq
    vx_ref[:, 0:MLA_V] = v_ref[...]
    vx_ref[:, MLA_V:] = jnp.ones((per, MLA_V), BF16)

    def scores(r0, nq, s_ref, nk):
        s_ref[0:nq, 0:nk] = lax.dot_general(q_ref[r0:r0 + nq, :], k_ref[0:nk, :], (((1,), (1,)), ((), ())),
                                            preferred_element_type=F32)

    def finish(r0, nq, s_ref, nk):
        s = s_ref[0:nq, 0:nk]
        p = jnp.exp2(s - jnp.max(s, axis=-1, keepdims=True))
        ox = jnp.dot(p.astype(BF16), vx_ref[0:nk, :], preferred_element_type=F32)
        o = ox[:, :MLA_V] / ox[:, MLA_V:]
        g = g_ref[r0:r0 + nq, :].astype(F32)
        o_ref[r0:r0 + nq, :] = (o * _silu(g)).astype(o_ref.dtype)

    if with_ctx:
        scores(0, n_ctx, s0_ref, n_ctx)
        finish(0, n_ctx, s0_ref, n_ctx)
    else:
        o_ref[0:n_ctx, :] = jnp.zeros((n_ctx, MLA_V), o_ref.dtype)

    s_refs = (s0_ref, s1_ref)
    scores(n_ctx, tq, s0_ref, per)
    for j in range(n_lat):
        if j + 1 < n_lat:
            scores(n_ctx + (j + 1) * tq, tq, s_refs[(j + 1) % 2], per)
        finish(n_ctx + j * tq, tq, s_refs[j % 2], per)


def _attention(q, k, v, gate, bsz, n_ctx, with_ctx):
    heads, r, dk = q.shape
    per = r // bsz
    tq = min(ATTN_Q_ROWS, per - n_ctx)
    assert (per - n_ctx) % tq == 0 and n_ctx <= tq
    blk = lambda n: pl.BlockSpec((None, per, n), lambda b, h: (h, b, 0))
    return pl.pallas_call(
        functools.partial(_attn_kernel, with_ctx=with_ctx, n_ctx=n_ctx, tq=tq),
        grid=(bsz, heads),
        in_specs=[blk(dk), blk(dk), blk(MLA_V), pl.BlockSpec((per, MLA_V), lambda b, h: (b, h))],
        out_specs=pl.BlockSpec((per, MLA_V), lambda b, h: (b, h)),
        out_shape=jax.ShapeDtypeStruct((r, heads * MLA_V), BF16),
        scratch_shapes=[pltpu.VMEM((per, 2 * MLA_V), BF16), pltpu.VMEM((tq, per), F32), pltpu.VMEM((tq, per), F32)],
        compiler_params=_cparams(56, "parallel", "parallel"),
        name="mla_attention",
    )(q, k, v, gate)


def _s5_matrices(a_re, a_im, bb_re, bb_im, c_re, c_im):
    gps = LANE // S5_GROUP
    ne, nd, g, p, h = bb_re.shape
    slabs = g // gps
    eye = jnp.eye(gps, dtype=F32)
    bd_in = lambda m: (jnp.swapaxes(m.reshape(ne, nd, slabs, gps, p, h), -1, -2)[..., None, :]
                       * eye[:, None, :, None]).reshape(ne, nd, slabs, gps * h, gps * p)
    bd_out = lambda m: (jnp.transpose(m.reshape(ne, nd, slabs, gps, h, p), (0, 1, 2, 5, 3, 4))[:, :, :, None]
                        * eye[:, None, :, None]).reshape(ne, nd, slabs, gps * p, gps * h)
    bm = jnp.concatenate([bd_in(bb_re), bd_in(bb_im)], axis=-1).astype(BF16)
    cm = jnp.concatenate([bd_out(c_re), -bd_out(c_im)], axis=-2).astype(BF16)
    ar = a_re.reshape(ne, nd, slabs, 1, gps * p)
    ai = a_im.reshape(ne, nd, slabs, 1, gps * p)
    return bm, cm, ar, ai


def _rope_table(n_ctx, n_lat):
    axis = MLA_ROPE // 2
    rows = n_lat // GRID_W
    row = jnp.repeat(jnp.arange(rows, dtype=F32), GRID_W)
    col = jnp.tile(jnp.arange(GRID_W, dtype=F32), rows)
    inv = ROPE_BASE ** (-jnp.arange(0, axis, 2, dtype=F32) / axis)
    ang = jnp.concatenate([row[:, None] * inv, col[:, None] * inv], axis=-1)
    cos, sin = jnp.cos(ang), jnp.sin(ang)
    lat = jnp.concatenate([cos, cos, -sin, sin], axis=-1)
    one, zero = jnp.ones((n_ctx, axis), F32), jnp.zeros((n_ctx, axis), F32)
    return jnp.concatenate([jnp.concatenate([one, one, zero, zero], axis=-1), lat], axis=0)


def _swap_halves(w):
    half = w.shape[-1] // 2
    return jnp.concatenate([w[..., half:], w[..., :half]], axis=-1)


def kernel(x, c, ctx, c_ctx, norm_g, mod_w, mod_b, ev_w_in, lru_conv_w, lru_conv_b, lru_wr, lru_br, lru_wi, lru_bi, lru_lam, s5_lam_re, s5_lam_im, s5_log_dt, s5_b_re, s5_b_im, s5_c_re, s5_c_im, s5_d, s5_glu_w, s5_glu_b, ev_w_out, mla_w_in, mla_q_norm, mla_w_uq, mla_kv_norm, mla_w_ukv, mla_w_out, final_g):
    bsz, n_lat, d = x.shape
    n_ctx = ctx.shape[1]
    steps = n_ctx + n_lat
    depth = mod_w.shape[0]
    n_even, n_odd = ev_w_in.shape[0], mla_w_in.shape[0]
    assert bsz == SUBLANE, "time-major layout maps the batch onto the sublanes of one vreg"
    assert depth % 2 == 0, "the last layer is an attention layer"
    lru_w = lru_conv_w.shape[-1]
    s5_w = s5_d.shape[-2] * s5_d.shape[-1]
    nh = lru_w // LRU_HEAD_DIM
    heads = MLA_HEADS
    q_rank, kv_rank = mla_q_norm.shape[-1], mla_kv_norm.shape[-1]

    cond = jnp.concatenate([c, jnp.broadcast_to(c_ctx[None], (bsz, d))], axis=0)
    mod5 = _modulation(cond, mod_w, mod_b).reshape(depth, 2, bsz, 1, 3 * d)
    norm_g3 = norm_g.reshape(depth, 1, d)

    ev_w_in_b = ev_w_in.astype(BF16)
    ev_w_out_b = ev_w_out.astype(BF16)
    glu_w_b = s5_glu_w.astype(BF16)
    glu_b3 = s5_glu_b.reshape(n_even, 1, s5_w)
    conv_b3 = lru_conv_b.reshape(n_even, 1, lru_w)
    wg = (0.5 * jnp.concatenate([lru_wr, lru_wi], axis=-1)).astype(BF16)
    bg = 0.5 * jnp.concatenate([lru_br.reshape(n_even, 2, nh, LRU_HEAD_DIM), lru_bi.reshape(n_even, 2, nh, LRU_HEAD_DIM)],
                               axis=-1).reshape(n_even, 2, 1, 2 * lru_w)
    lam4 = lru_lam.reshape(n_even, 2, 1, lru_w)
    a_re, a_im, bb_re, bb_im = _s5_discretise(s5_lam_re, s5_lam_im, s5_log_dt, s5_b_re, s5_b_im)
    s5_bm, s5_cm, s5_ar, s5_ai = _s5_matrices(a_re, a_im, bb_re, bb_im, s5_c_re, s5_c_im)
    dsk3 = s5_d.reshape(n_even, 1, s5_w)

    off_kr, off_g = q_rank + kv_rank, q_rank + kv_rank + MLA_ROPE
    w_kr = mla_w_in[..., off_kr:off_g]
    mla_w_in_b = jnp.concatenate([mla_w_in[..., :off_kr], mla_w_in[..., off_g:], w_kr, _swap_halves(w_kr)],
                                 axis=-1).astype(BF16)
    wq = mla_w_uq.reshape(n_odd, q_rank, heads, MLA_NOPE + MLA_ROPE)
    wq_b = jnp.concatenate([wq, _swap_halves(wq[..., MLA_NOPE:])], axis=-1).reshape(n_odd, q_rank, -1).astype(BF16)
    wkv_b = mla_w_ukv.astype(BF16)
    mla_w_out_b = mla_w_out.astype(BF16)
    q_norm3 = mla_q_norm.reshape(n_odd, 1, q_rank)
    kv_norm3 = mla_kv_norm.reshape(n_odd, 1, kv_rank)
    rope_tab = _rope_table(n_ctx, n_lat)

    ts = PROJ_STEPS
    ctx_blocks = n_ctx // ts
    seg_tm = lambda i: (i < ctx_blocks).astype(jnp.int32)
    tq = n_ctx
    blocks_per_sample = steps // tq
    rows = bsz * steps
    tm_odd = ODD_ROWS if steps % ODD_ROWS == 0 and n_ctx < ODD_ROWS else tq
    assert steps % tm_odd == 0 and n_ctx <= tm_odd

    res = (ctx, x)
    for l in range(depth):
        if l % 2 == 0:
            e = l // 2
            xa, ga, ub, gb = _inproj(res, mod5, norm_g3, ev_w_in_b, l, e, seg_tm,
                                     (lru_w, lru_w, s5_w, s5_w), (BF16,) * 4, ts, ctx_blocks, steps)
            ya = _lru_mix(xa, ga, lru_conv_w, conv_b3, wg, bg, lam4, e, n_ctx * bsz)
            yg = _s5_mix(ub, s5_bm, s5_cm, s5_ar, s5_ai, dsk3, e, n_ctx * bsz)
            xs = _outproj_even(ya, yg, gb, res, mod5, l, seg_tm, glu_w_b, glu_b3, ev_w_out_b, e, ts, ctx_blocks, steps)
            res = (xs,)
        else:
            o = l // 2
            with_ctx = l < depth - 1
            xs2 = xs.reshape(rows, d)
            gate, q, k, v = _mla_in(xs2, mod5, norm_g3, mla_w_in_b, rope_tab, q_norm3, kv_norm3, wq_b, wkv_b,
                                    l, o, tm_odd, steps // tm_odd, n_ctx)
            att = _attention(q, k, v, gate, bsz, n_ctx, with_ctx)
            if with_ctx:
                xs = _outproj_odd(att, xs2, mod5, l, mla_w_out_b, o, tm_odd, steps // tm_odd, bsz,
                                  n_ctx=n_ctx).reshape(bsz, steps, d)
                res = (xs,)
            else:
                xs = _outproj_odd(att, xs2, mod5, l, mla_w_out_b, o, tq, blocks_per_sample, bsz, final_g=final_g)
    return xs
```
